```python
import math
import jax
import jax.numpy as jnp
from jax import lax
import numpy as np

D_MODEL = 1024
BATCH = 4
SEQ = 8192
DEPTH = 2

D_MIX = D_MODEL
POOL_W = D_MIX // 4
POOL_WINDOWS = (2, 4, 8, 16)
POOL_GROUP = POOL_W // len(POOL_WINDOWS)
DIFF_HEADS = 4
DIFF_QK_DIM = 32
DIFF_V_DIM = 2 * DIFF_QK_DIM
DIFF_W = DIFF_HEADS * DIFF_V_DIM
Q_BLOCK = 128
SSD_INNER = D_MIX - POOL_W - DIFF_W
SSD_HEAD_DIM = 64
SSD_HEADS = SSD_INNER // SSD_HEAD_DIM
SSD_GROUPS = 2
SSD_HEADS_PER_GROUP = SSD_HEADS // SSD_GROUPS
SSD_STATE = 128
CONV_K = 4
CONV_CH = SSD_INNER + 2 * SSD_GROUPS * SSD_STATE
CHUNK = 128
D_FF = 4 * D_MODEL
PLE_DIM = 256
ALPHA = (2 * DEPTH) ** 0.25
BETA = (8 * DEPTH) ** -0.25
LN_EPS = 1e-5
IN_SIZES = (POOL_W, DIFF_HEADS * 2 * DIFF_QK_DIM, DIFF_HEADS * 2 * DIFF_QK_DIM, DIFF_W,
            SSD_INNER, CONV_CH, SSD_HEADS)
IN_W = sum(IN_SIZES)

kernel_name = 'hymba_style_pool_diffattn_ssd_deepnorm'


def layer_norm(x, g, b):
    xf = x.astype(jnp.float32)
    mu = jnp.mean(xf, axis=-1, keepdims=True)
    xc = xf - mu
    var = jnp.mean(xc * xc, axis=-1, keepdims=True)
    y = xc * lax.rsqrt(var + LN_EPS) * g.astype(jnp.float32) + b.astype(jnp.float32)
    return y.astype(x.dtype)


def rms_norm(x, g):
    xf = x.astype(jnp.float32)
    y = xf * lax.rsqrt(jnp.mean(xf * xf, axis=-1, keepdims=True) + LN_EPS)
    return y * g.astype(jnp.float32)


def split_cols(a, sizes):
    out = []
    o = 0
    for s in sizes:
        out.append(a[..., o:o + s])
        o += s
    return out


def multiscale_pool(u, w_grp, scale):
    Bsz, S, _ = u.shape
    uf = u.astype(jnp.float32)
    cs = jnp.cumsum(uf, axis=1)
    pos = jnp.arange(1, S + 1, dtype=jnp.float32)
    outs = []
    for gi, w in enumerate(POOL_WINDOWS):
        sl = slice(gi * POOL_GROUP, (gi + 1) * POOL_GROUP)
        csg = cs[..., sl]
        lag = jnp.pad(csg, ((0, 0), (w, 0), (0, 0)))[:, :S]
        cnt = jnp.minimum(pos, float(w))[None, :, None]
        outs.append((csg - lag) / cnt - uf[..., sl])
    pooled = jnp.stack(outs, axis=2)
    mixed = jnp.einsum('bsgc,gcd->bsgd', pooled, w_grp.astype(jnp.float32)).reshape(Bsz, S, POOL_W)
    return (mixed * scale.astype(jnp.float32)).astype(u.dtype)


def diff_attention(q, k, v, lam_q1, lam_k1, lam_q2, lam_k2, norm_g, lambda_init):
    Bsz, S, _ = q.shape
    dtype = q.dtype
    q = q.reshape(Bsz, S, DIFF_HEADS, 2, DIFF_QK_DIM).transpose(0, 2, 3, 1, 4) * (DIFF_QK_DIM ** -0.5)
    k = k.reshape(Bsz, S, DIFF_HEADS, 2, DIFF_QK_DIM).transpose(0, 2, 3, 1, 4)
    vf = v.reshape(Bsz, S, DIFF_HEADS, DIFF_V_DIM).transpose(0, 2, 1, 3).astype(jnp.float32)
    f32 = jnp.float32
    lam = (jnp.exp(jnp.sum(lam_q1.astype(f32) * lam_k1.astype(f32)))
           - jnp.exp(jnp.sum(lam_q2.astype(f32) * lam_k2.astype(f32))) + lambda_init)
    nblk = S // Q_BLOCK
    qb = q.reshape(Bsz, DIFF_HEADS, 2, nblk, Q_BLOCK, DIFF_QK_DIM).transpose(3, 0, 1, 2, 4, 5)
    kpos = jnp.arange(S)

    def block(args):
        qi, bi = args
        s = jnp.einsum('bhjqd,bhjkd->bhjqk', qi, k).astype(f32)
        qpos = bi * Q_BLOCK + jnp.arange(Q_BLOCK)
        mask = kpos[None, :] <= qpos[:, None]
        s = jnp.where(mask, s, -jnp.inf)
        a = jax.nn.softmax(s, axis=-1)
        w = a[:, :, 0] - lam * a[:, :, 1]
        return jnp.einsum('bhqk,bhkv->bhqv', w, vf)

    o = lax.map(block, (qb, jnp.arange(nblk)))
    o = o.transpose(1, 0, 3, 2, 4).reshape(Bsz, S, DIFF_HEADS, DIFF_V_DIM)
    o = rms_norm(o, norm_g) * (1.0 - lambda_init)
    return o.reshape(Bsz, S, DIFF_W).astype(dtype)


def causal_depthwise_conv(x, w, b):
    C = x.shape[-1]
    y = lax.conv_general_dilated(x, w[:, None, :].astype(x.dtype), window_strides=(1,),
                                 padding=((CONV_K - 1, 0),), dimension_numbers=('NWC', 'WIO', 'NWC'),
                                 feature_group_count=C)
    return y + b.astype(x.dtype)


def ssd_mixer(z, xbc, dt_raw, conv_w, conv_b, dt_bias, a_log, d_skip, norm_g):
    Bsz, S, _ = xbc.shape
    dtype = xbc.dtype
    f32 = jnp.float32
    G, R, P, N, Lc = SSD_GROUPS, SSD_HEADS_PER_GROUP, SSD_HEAD_DIM, SSD_STATE, CHUNK
    nc = S // Lc
    xbc = jax.nn.silu(causal_depthwise_conv(xbc, conv_w, conv_b))
    xs, Bm, Cm = split_cols(xbc, (SSD_INNER, G * N, G * N))
    xs = xs.astype(f32).reshape(Bsz, nc, Lc, G, R, P)
    Bm = Bm.astype(f32).reshape(Bsz, nc, Lc, G, N)
    Cm = Cm.astype(f32).reshape(Bsz, nc, Lc, G, N)
    dt = jax.nn.softplus(dt_raw.astype(f32) + dt_bias.astype(f32)).reshape(Bsz, nc, Lc, G, R)
    A = -jnp.exp(a_log.astype(f32)).reshape(G, R)
    xdt = xs * dt[..., None]
    acs = jnp.cumsum(dt * A, axis=2).transpose(0, 1, 3, 4, 2)
    seg = acs[..., :, None] - acs[..., None, :]
    tril = jnp.tril(jnp.ones((Lc, Lc), dtype=bool))
    Lmat = jnp.exp(jnp.where(tril, seg, -jnp.inf))
    CB = jnp.einsum('bclgn,bcsgn->bcgls', Cm, Bm)
    y_diag = jnp.einsum('bcgls,bcgrls,bcsgrp->bclgrp', CB, Lmat, xdt)
    decay_states = jnp.exp(acs[..., -1:] - acs)
    states = jnp.einsum('bclgn,bcgrl,bclgrp->bcgrpn', Bm, decay_states, xdt)
    chunk_decay = jnp.exp(acs[..., -1])

    def step(h, inp):
        st, dec = inp
        return h * dec[..., None, None] + st, h

    h0 = jnp.zeros((Bsz, G, R, P, N), f32)
    _, prev = lax.scan(step, h0, (jnp.moveaxis(states, 1, 0), jnp.moveaxis(chunk_decay, 1, 0)))
    prev = jnp.moveaxis(prev, 0, 1)
    y_off = jnp.einsum('bclgn,bcgrpn,bcgrl->bclgrp', Cm, prev, jnp.exp(acs))
    y = y_diag + y_off + xs * d_skip.astype(f32).reshape(G, R)[:, :, None]
    y = y.reshape(Bsz, S, SSD_INNER) * jax.nn.silu(z.astype(f32))
    return rms_norm(y, norm_g).astype(dtype)


def setup_inputs(seed: int = 0) -> dict:
    key = jax.random.key(seed)
    ks = jax.random.split(key, 32)
    L = DEPTH
    f32 = jnp.float32

    def nrm(k, shape):
        return jax.random.normal(k, shape, f32)

    x = nrm(ks[0], (BATCH, SEQ, D_MODEL))
    p = nrm(ks[1], (DEPTH, BATCH, SEQ, PLE_DIM))
    ln_in_g = 1.0 + 0.02 * nrm(ks[2], (D_MODEL,))
    ln_in_b = 0.02 * nrm(ks[3], (D_MODEL,))
    w_in = nrm(ks[4], (L, D_MODEL, IN_W)) * (D_MODEL ** -0.5)
    pool_w = nrm(ks[5], (L, len(POOL_WINDOWS), POOL_GROUP, POOL_GROUP)) * (POOL_GROUP ** -0.5)
    pool_scale = 1.0 + 0.02 * nrm(ks[6], (L, POOL_W))
    lam_q1 = 0.1 * nrm(ks[7], (L, DIFF_QK_DIM))
    lam_k1 = 0.1 * nrm(ks[8], (L, DIFF_QK_DIM))
    lam_q2 = 0.1 * nrm(ks[9], (L, DIFF_QK_DIM))
    lam_k2 = 0.1 * nrm(ks[10], (L, DIFF_QK_DIM))
    diff_norm_g = 1.0 + 0.02 * nrm(ks[11], (L, DIFF_V_DIM))
    conv_w = nrm(ks[12], (L, CONV_K, CONV_CH)) * (CONV_K ** -0.5)
    conv_b = 0.02 * nrm(ks[13], (L, CONV_CH))
    dt0 = jnp.exp(jax.random.uniform(ks[14], (L, SSD_HEADS), f32, minval=math.log(1e-3), maxval=math.log(1e-1)))
    dt_bias = dt0 + jnp.log(-jnp.expm1(-dt0))
    a_log = jnp.log(jax.random.uniform(ks[15], (L, SSD_HEADS), f32, minval=1.0, maxval=16.0))
    d_skip = 1.0 + 0.02 * nrm(ks[16], (L, SSD_HEADS))
    ssd_norm_g = 1.0 + 0.02 * nrm(ks[17], (L, SSD_INNER))
    w_out = nrm(ks[18], (L, D_MIX, D_MODEL)) * (D_MIX ** -0.5 * BETA)
    ln1_g = 1.0 + 0.02 * nrm(ks[19], (L, D_MODEL))
    ln1_b = 0.02 * nrm(ks[20], (L, D_MODEL))
    w_ff1 = nrm(ks[21], (L, D_MODEL, D_FF)) * (D_MODEL ** -0.5)
    w_ff2 = nrm(ks[22], (L, D_FF, D_MODEL)) * (D_FF ** -0.5 * BETA)
    w_ple = nrm(ks[23], (L, PLE_DIM, D_MODEL)) * (PLE_DIM ** -0.5 * BETA)
    w_ple_gate = nrm(ks[24], (L, D_MODEL, D_MODEL)) * (D_MODEL ** -0.5)
    ln2_g = 1.0 + 0.02 * nrm(ks[25], (L, D_MODEL))
    ln2_b = 0.02 * nrm(ks[26], (L, D_MODEL))
    return {'x': x, 'p': p, 'ln_in_g': ln_in_g, 'ln_in_b': ln_in_b, 'w_in': w_in,
            'pool_w': pool_w, 'pool_scale': pool_scale,
            'lam_q1': lam_q1, 'lam_k1': lam_k1, 'lam_q2': lam_q2, 'lam_k2': lam_k2, 'diff_norm_g': diff_norm_g,
            'conv_w': conv_w, 'conv_b': conv_b, 'dt_bias': dt_bias, 'a_log': a_log, 'd_skip': d_skip,
            'ssd_norm_g': ssd_norm_g, 'w_out': w_out, 'ln1_g': ln1_g, 'ln1_b': ln1_b,
            'w_ff1': w_ff1, 'w_ff2': w_ff2, 'w_ple': w_ple, 'w_ple_gate': w_ple_gate,
            'ln2_g': ln2_g, 'ln2_b': ln2_b}


def reference(x, p, ln_in_g, ln_in_b, w_in, pool_w, pool_scale, lam_q1, lam_k1, lam_q2, lam_k2, diff_norm_g,
              conv_w, conv_b, dt_bias, a_log, d_skip, ssd_norm_g, w_out, ln1_g, ln1_b,
              w_ff1, w_ff2, w_ple, w_ple_gate, ln2_g, ln2_b):
    h = layer_norm(x, ln_in_g, ln_in_b)
    for i in range(DEPTH):
        lambda_init = 0.8 - 0.6 * math.exp(-0.3 * i)
        proj = h @ w_in[i]
        u_pool, q, k, v, z, xbc, dt_raw = split_cols(proj, IN_SIZES)
        o_pool = multiscale_pool(u_pool, pool_w[i], pool_scale[i])
        o_diff = diff_attention(q, k, v, lam_q1[i], lam_k1[i], lam_q2[i], lam_k2[i], diff_norm_g[i], lambda_init)
        o_ssd = ssd_mixer(z, xbc, dt_raw, conv_w[i], conv_b[i], dt_bias[i], a_log[i], d_skip[i], ssd_norm_g[i])
        mix = jnp.concatenate([o_pool, o_diff, o_ssd], axis=-1) @ w_out[i]
        h = layer_norm(ALPHA * h + mix, ln1_g[i], ln1_b[i])
        ff = jnp.square(jax.nn.relu(h @ w_ff1[i])) @ w_ff2[i]
        ple = (p[i] @ w_ple[i]) * jax.nn.sigmoid(h @ w_ple_gate[i])
        h = layer_norm(ALPHA * h + ff + ple, ln2_g[i], ln2_b[i])
    return h
```

```python
import functools
import math

import jax
import jax.numpy as jnp
from jax import lax
from jax.experimental import pallas as pl
from jax.experimental.pallas import tpu as pltpu

F32 = jnp.float32
BF16 = jnp.bfloat16

POOL_WINDOWS = (2, 4, 8, 16)
POOL_GROUP = 64
POOL_W = POOL_GROUP * len(POOL_WINDOWS)
DIFF_HEADS = 4
DIFF_QK_DIM = 32
DIFF_V_DIM = 2 * DIFF_QK_DIM
DIFF_W = DIFF_HEADS * DIFF_V_DIM
SSD_INNER = 512
SSD_HEAD_DIM = 64
SSD_HEADS = SSD_INNER // SSD_HEAD_DIM
SSD_GROUPS = 2
SSD_HEADS_PER_GROUP = SSD_HEADS // SSD_GROUPS
SSD_STATE = 128
CONV_K = 4
CONV_CH = SSD_INNER + 2 * SSD_GROUPS * SSD_STATE
CHUNK = 128
LN_EPS = 1e-5
LOG2E = 1.4426950408889634

LANES = 128
SUBLANES = 8
VMEM_LIMIT = 56 * 1024 * 1024

ROW_TILE = 512
ATT_TQ = 256
ATT_TK = 256
POOL_HALO = 16

_OFF_POOL = 0
_OFF_Q = _OFF_POOL + POOL_W
_OFF_K = _OFF_Q + DIFF_W
_OFF_V = _OFF_K + DIFF_W
_OFF_Z = _OFF_V + DIFF_W
_OFF_XBC = _OFF_Z + SSD_INNER
_OFF_DT = _OFF_XBC + CONV_CH
_AUG_W = _OFF_DT + SSD_INNER


def _layer_norm(x, g, b):
    mu = jnp.mean(x, axis=-1, keepdims=True)
    xc = x - mu
    var = jnp.mean(xc * xc, axis=-1, keepdims=True)
    return xc * lax.rsqrt(var + LN_EPS) * g + b


def _const_spec(shape):
    nd = len(shape)
    return pl.BlockSpec(shape, lambda *_: (0,) * nd, pipeline_mode=pl.Buffered(1))


def _params(n_axes):
    return pltpu.CompilerParams(dimension_semantics=("arbitrary",) * n_axes, vmem_limit_bytes=VMEM_LIMIT)


def _inproj_kernel(apply_ln, qscale, x_ref, g_ref, b_ref, w_ref, *out_refs):
    if apply_ln:
        h_ref, up_ref, q_ref, k_ref, vt_ref, z_ref, xbc_ref, dt_ref = out_refs
        x = _layer_norm(x_ref[...], g_ref[...], b_ref[...])
        h_ref[...] = x
    else:
        up_ref, q_ref, k_ref, vt_ref, z_ref, xbc_ref, dt_ref = out_refs
        x = x_ref[...]
    xb = x.astype(BF16)

    def proj(lo, width):
        return jnp.dot(xb, w_ref[:, lo:lo + width], preferred_element_type=F32)

    up_ref[...] = proj(_OFF_POOL, POOL_W)
    q_ref[...] = (proj(_OFF_Q, DIFF_W) * qscale).astype(BF16)
    k_ref[...] = proj(_OFF_K, DIFF_W).astype(BF16)
    v = proj(_OFF_V, DIFF_W)
    tk = vt_ref.shape[-1]
    for s in range(vt_ref.shape[0]):
        vt_ref[s] = v[s * tk:(s + 1) * tk, :].T.astype(BF16)
    z_ref[...] = proj(_OFF_Z, SSD_INNER)
    xbc_ref[...] = proj(_OFF_XBC, CONV_CH)
    dt_ref[...] = proj(_OFF_DT, SSD_INNER)


def _in_projection(x, ln_g, ln_b, w_aug, apply_ln):
    t, d = x.shape
    tm = ROW_TILE
    nsub = tm // ATT_TK
    qscale = DIFF_QK_DIM ** -0.5 * LOG2E
    row = lambda w: pl.BlockSpec((tm, w), lambda i: (i, 0))
    out_shape = [
        jax.ShapeDtypeStruct((t, POOL_W), F32),
        jax.ShapeDtypeStruct((t, DIFF_W), BF16),
        jax.ShapeDtypeStruct((t, DIFF_W), BF16),
        jax.ShapeDtypeStruct((t // ATT_TK, DIFF_W, ATT_TK), BF16),
        jax.ShapeDtypeStruct((t, SSD_INNER), F32),
        jax.ShapeDtypeStruct((t, CONV_CH), F32),
        jax.ShapeDtypeStruct((t, SSD_INNER), F32),
    ]
    out_specs = [row(POOL_W), row(DIFF_W), row(DIFF_W),
                 pl.BlockSpec((nsub, DIFF_W, ATT_TK), lambda i: (i, 0, 0)),
                 row(SSD_INNER), row(CONV_CH), row(SSD_INNER)]
    if apply_ln:
        out_shape = [jax.ShapeDtypeStruct((t, d), F32)] + out_shape
        out_specs = [row(d)] + out_specs
    return pl.pallas_call(
        functools.partial(_inproj_kernel, apply_ln, qscale),
        grid=(t // tm,),
        in_specs=[row(d), _const_spec((1, d)), _const_spec((1, d)), _const_spec(w_aug.shape)],
        out_specs=out_specs,
        out_shape=out_shape,
        compiler_params=_params(1),
        name="in_proj_ln" if apply_ln else "in_proj",
    )(x, ln_g, ln_b, w_aug)


def _attn_kernel(lambda_init, q_ref, k_ref, vt_ref, lamv_ref, ng_ref, o_ref, acc_ref):
    qi = pl.program_id(2)
    tq = q_ref.shape[0]
    tk = vt_ref.shape[-1]
    n_comp = LANES // DIFF_QK_DIM
    qb = q_ref[...]
    lane = lax.broadcasted_iota(jnp.int32, (1, LANES), 1)
    zero = jnp.zeros_like(qb)
    qm = [jnp.where((lane >= DIFF_QK_DIM * c) & (lane < DIFF_QK_DIM * (c + 1)), qb, zero) for c in range(n_comp)]
    acc_ref[...] = jnp.zeros_like(acc_ref)

    def block(j, carry, masked):
        ms, ls = carry
        kb = k_ref[pl.ds(pl.multiple_of(j * tk, tk), tk), :]
        vb = vt_ref[j]
        if masked:
            key_pos = j * tk + lax.broadcasted_iota(jnp.int32, (tk, tq), 0)
            q_pos = qi * tq + lax.broadcasted_iota(jnp.int32, (tk, tq), 1)
            keep = key_pos <= q_pos
        new_ms, new_ls = [], []
        for c in range(n_comp):
            s = lax.dot_general(kb, qm[c], (((1,), (1,)), ((), ())), preferred_element_type=F32)
            if masked:
                s = jnp.where(keep, s, -jnp.inf)
            m_new = jnp.maximum(ms[c], jnp.max(s, axis=0, keepdims=True))
            alpha = jnp.exp2(ms[c] - m_new)
            p = jnp.exp2(s - m_new)
            new_ls.append(alpha * ls[c] + jnp.sum(p, axis=0, keepdims=True))
            new_ms.append(m_new)
            acc_ref[c] = acc_ref[c] * alpha + jnp.dot(vb, p.astype(BF16), preferred_element_type=F32)
        return tuple(new_ms), tuple(new_ls)

    init = (tuple(jnp.full((1, tq), -jnp.inf, F32) for _ in range(n_comp)),
            tuple(jnp.zeros((1, tq), F32) for _ in range(n_comp)))
    nfull = (qi * tq) // tk
    carry = lax.fori_loop(0, nfull, functools.partial(block, masked=False), init)
    for d in range(tq // tk):
        carry = block(nfull + d, carry, True)
    _, ls = carry

    lamv = lamv_ref[...]
    lam = (jnp.exp(jnp.sum(lamv[0:1] * lamv[1:2], axis=-1, keepdims=True))
           - jnp.exp(jnp.sum(lamv[2:3] * lamv[3:4], axis=-1, keepdims=True)) + lambda_init)
    row = lax.broadcasted_iota(jnp.int32, (LANES, tq), 0)
    heads = []
    for h in range(2):
        c1, c2 = 2 * h, 2 * h + 1
        heads.append(acc_ref[c1] * (1.0 / ls[c1]) - lam * (acc_ref[c2] * (1.0 / ls[c2])))
    first = row < DIFF_V_DIM
    ot = jnp.where(first, heads[0], heads[1])
    sq = ot * ot
    ms0 = jnp.sum(jnp.where(first, sq, 0.0), axis=0, keepdims=True) * (1.0 / DIFF_V_DIM)
    ms1 = jnp.sum(jnp.where(first, 0.0, sq), axis=0, keepdims=True) * (1.0 / DIFF_V_DIM)
    inv = jnp.where(first, lax.rsqrt(ms0 + LN_EPS), lax.rsqrt(ms1 + LN_EPS))
    y = (ot * inv).T * ng_ref[...] * (1.0 - lambda_init)
    o_ref[...] = y


def _diff_attention(q, k, vt, lamv, ng2, batch, seq, lambda_init):
    t = q.shape[0]
    tq, tk = ATT_TQ, ATT_TK
    nq = seq // tq
    nkb = seq // tk
    n_pairs = DIFF_W // LANES
    return pl.pallas_call(
        functools.partial(_attn_kernel, lambda_init),
        grid=(batch, n_pairs, nq),
        in_specs=[
            pl.BlockSpec((tq, LANES), lambda b, hp, i: (b * nq + i, hp)),
            pl.BlockSpec((seq, LANES), lambda b, hp, i: (b, hp)),
            pl.BlockSpec((nkb, LANES, tk), lambda b, hp, i: (b, hp, 0)),
            _const_spec(lamv.shape),
            _const_spec(ng2.shape),
        ],
        out_specs=pl.BlockSpec((tq, LANES), lambda b, hp, i: (b * nq + i, hp)),
        out_shape=jax.ShapeDtypeStruct((t, DIFF_W), F32),
        scratch_shapes=[pltpu.VMEM((LANES // DIFF_QK_DIM, LANES, tq), F32)],
        compiler_params=_params(3),
        name="diff_attention",
    )(q, k, vt, lamv, ng2)


def _split3(x):
    hi = x.astype(BF16)
    r = x - hi.astype(F32)
    mid = r.astype(BF16)
    lo = (r - mid.astype(F32)).astype(BF16)
    return hi, mid, lo


def _ssd_kernel(xbc_ref, z_ref, dt_ref, cw_ref, cb_ref, dtb_ref, alog_ref, dsk_ref, ng_ref, o_ref,
                ext_ref, state_ref):
    c = pl.program_id(1)
    lc = xbc_ref.shape[0]
    g_w = SSD_HEADS_PER_GROUP * SSD_HEAD_DIM
    n = SSD_STATE

    @pl.when(c == 0)
    def _():
        state_ref[...] = jnp.zeros_like(state_ref)
        ext_ref[0:SUBLANES, :] = jnp.zeros((SUBLANES, CONV_CH), F32)

    xin = xbc_ref[...]
    ext_ref[SUBLANES:, :] = xin
    cw = cw_ref[...]
    conv = cb_ref[...] + cw[CONV_K - 1:CONV_K] * xin
    for j in range(CONV_K - 1):
        off = SUBLANES - (CONV_K - 1) + j
        conv = conv + cw[j:j + 1] * ext_ref[off:off + lc, :]
    ext_ref[0:SUBLANES, :] = xin[lc - SUBLANES:, :]
    act = conv * jax.nn.sigmoid(conv)
    xs = act[:, :SSD_INNER]
    bm = act[:, SSD_INNER:SSD_INNER + SSD_GROUPS * n]
    cm = act[:, SSD_INNER + SSD_GROUPS * n:]

    dt_in = dt_ref[...] + dtb_ref[...]
    dt = jnp.maximum(dt_in, 0.0) + jnp.log(1.0 + jnp.exp(-jnp.abs(dt_in)))
    a_neg = -jnp.exp(alog_ref[...])
    dta = dt * a_neg
    ri = lax.broadcasted_iota(jnp.int32, (lc, lc), 0)
    ci = lax.broadcasted_iota(jnp.int32, (lc, lc), 1)
    causal = ri >= ci
    tri = jnp.where(causal, 1.0, 0.0).astype(BF16)
    acs = None
    for part in _split3(dta):
        term = jnp.dot(tri, part, preferred_element_type=F32)
        acs = term if acs is None else acs + term
    acs_last = acs[lc - 1:lc, :]
    xdt = xs * dt
    decayed = (xdt * jnp.exp(acs_last - acs)).astype(BF16)
    xdt_b = xdt.astype(BF16)
    exp_acs = jnp.exp(acs)
    chunk_decay = jnp.exp(acs_last)

    lane_g = lax.broadcasted_iota(jnp.int32, (1, g_w), 1)
    lane128 = lax.broadcasted_iota(jnp.int32, (1, LANES), 1)
    y_parts = []
    for g in range(SSD_GROUPS):
        bm_g = bm[:, g * n:(g + 1) * n]
        cm_g = cm[:, g * n:(g + 1) * n].astype(BF16)
        cb = lax.dot_general(cm_g, bm_g.astype(BF16), (((1,), (1,)), ((), ())), preferred_element_type=F32)
        xdt_g = xdt_b[:, g * g_w:(g + 1) * g_w]
        y_g = None
        for pair in range(SSD_HEADS_PER_GROUP // 2):
            lo = g * g_w + pair * LANES
            a_pair = acs[:, lo:lo + LANES]
            a_swap = pltpu.roll(a_pair, DIFF_V_DIM, 1)
            a_t = a_pair.T
            for sub in range(2):
                r = 2 * pair + sub
                own = (lane128 < SSD_HEAD_DIM) if sub == 0 else (lane128 >= SSD_HEAD_DIM)
                col = jnp.where(own, a_pair, a_swap)
                rowv = a_t[sub * SSD_HEAD_DIM:sub * SSD_HEAD_DIM + 1, :]
                lmat = jnp.where(causal, jnp.exp(col - rowv), 0.0)
                w = (cb * lmat).astype(BF16)
                sel = (lane_g >= r * SSD_HEAD_DIM) & (lane_g < (r + 1) * SSD_HEAD_DIM)
                contrib = jnp.dot(w, jnp.where(sel, xdt_g, jnp.zeros_like(xdt_g)), preferred_element_type=F32)
                y_g = contrib if y_g is None else y_g + contrib
        prev = state_ref[g]
        y_off = jnp.dot(cm_g, prev.astype(BF16), preferred_element_type=F32) * exp_acs[:, g * g_w:(g + 1) * g_w]
        st = jnp.dot(bm_g.T.astype(BF16), decayed[:, g * g_w:(g + 1) * g_w], preferred_element_type=F32)
        state_ref[g] = prev * chunk_decay[:, g * g_w:(g + 1) * g_w] + st
        y_parts.append(y_g + y_off)
    y = jnp.concatenate(y_parts, axis=1) + xs * dsk_ref[...]
    zz = z_ref[...]
    y = y * (zz * jax.nn.sigmoid(zz))
    y = y * lax.rsqrt(jnp.mean(y * y, axis=-1, keepdims=True) + LN_EPS)
    o_ref[...] = y * ng_ref[...]


def _ssd_mixer(xbc, z, dtf, conv_w, conv_b, dtb_f, alog_f, dsk_f, norm_g, batch, seq):
    t = xbc.shape[0]
    nc = seq // CHUNK
    row = lambda w: pl.BlockSpec((CHUNK, w), lambda b, c: (b * nc + c, 0))
    return pl.pallas_call(
        _ssd_kernel,
        grid=(batch, nc),
        in_specs=[row(CONV_CH), row(SSD_INNER), row(SSD_INNER),
                  _const_spec(conv_w.shape), _const_spec(conv_b.shape), _const_spec(dtb_f.shape),
                  _const_spec(alog_f.shape), _const_spec(dsk_f.shape), _const_spec(norm_g.shape)],
        out_specs=row(SSD_INNER),
        out_shape=jax.ShapeDtypeStruct((t, SSD_INNER), F32),
        scratch_shapes=[pltpu.VMEM((SUBLANES + CHUNK, CONV_CH), F32),
                        pltpu.VMEM((SSD_GROUPS, SSD_STATE, SSD_HEADS_PER_GROUP * SSD_HEAD_DIM), F32)],
        compiler_params=_params(2),
        name="ssd_mixer",
    )(xbc, z, dtf, conv_w, conv_b, dtb_f, alog_f, dsk_f, norm_g)


def _outproj_kernel(alpha, blocks_per_seq, u_ref, halo_ref, od_ref, os_ref, h_ref, pw_ref, ps_ref, wo_ref,
                    g_ref, b_ref, o_ref, lvl_ref):
    i = pl.program_id(0)
    tm = u_ref.shape[0]
    base = 2 * POOL_HALO
    seq_blk = i % blocks_per_seq
    u = u_ref[...]
    halo = halo_ref[...]
    lvl_ref[:, 0:POOL_HALO, :] = jnp.zeros((len(POOL_WINDOWS), POOL_HALO, POOL_W), F32)
    lvl_ref[0, POOL_HALO:base, :] = jnp.where(seq_blk == 0, jnp.zeros_like(halo), halo)
    lvl_ref[0, base:, :] = u
    n_rows = tm + POOL_HALO
    sums = []
    for lv, shift in enumerate((1, 2, 4, 8)):
        cur = lvl_ref[lv, POOL_HALO:, :] + lvl_ref[lv, POOL_HALO - shift:POOL_HALO - shift + n_rows, :]
        if lv + 1 < len(POOL_WINDOWS):
            lvl_ref[lv + 1, POOL_HALO:, :] = cur
        sums.append(cur[POOL_HALO:, :])
    pos = (seq_blk * tm + lax.broadcasted_iota(jnp.int32, (tm, 1), 0) + 1).astype(F32)
    lane = lax.broadcasted_iota(jnp.int32, (1, POOL_W), 1)
    pooled = jnp.zeros((tm, POOL_W), F32)
    for gi, w in enumerate(POOL_WINDOWS):
        mean = sums[gi] / jnp.minimum(pos, float(w))
        pooled = jnp.where((lane >= gi * POOL_GROUP) & (lane < (gi + 1) * POOL_GROUP), mean, pooled)
    pooled = pooled - u
    o_pool = jnp.dot(pooled.astype(BF16), pw_ref[...], preferred_element_type=F32) * ps_ref[...]
    cat = jnp.concatenate([o_pool.astype(BF16), od_ref[...].astype(BF16), os_ref[...].astype(BF16)], axis=1)
    mix = jnp.dot(cat, wo_ref[...], preferred_element_type=F32)
    o_ref[...] = _layer_norm(alpha * h_ref[...] + mix, g_ref[...], b_ref[...])


def _out_projection(u_pool, o_diff, o_ssd, h, pool_bd, pool_scale, w_out, g, b, seq, alpha):
    t, d = h.shape
    tm = ROW_TILE
    blocks_per_seq = seq // tm
    halo_per_tile = tm // POOL_HALO
    row = lambda w: pl.BlockSpec((tm, w), lambda i: (i, 0))
    return pl.pallas_call(
        functools.partial(_outproj_kernel, alpha, blocks_per_seq),
        grid=(t // tm,),
        in_specs=[row(POOL_W),
                  pl.BlockSpec((POOL_HALO, POOL_W), lambda i: (jnp.maximum(i * halo_per_tile - 1, 0), 0)),
                  row(DIFF_W), row(SSD_INNER), row(d),
                  _const_spec(pool_bd.shape), _const_spec(pool_scale.shape), _const_spec(w_out.shape),
                  _const_spec(g.shape), _const_spec(b.shape)],
        out_specs=row(d),
        out_shape=jax.ShapeDtypeStruct((t, d), F32),
        scratch_shapes=[pltpu.VMEM((len(POOL_WINDOWS), tm + 2 * POOL_HALO, POOL_W), F32)],
        compiler_params=_params(1),
        name="out_proj",
    )(u_pool, u_pool, o_diff, o_ssd, h, pool_bd, pool_scale, w_out, g, b)


def _ffn_kernel(alpha, h_ref, p_ref, w1_ref, w2_ref, wp_ref, wg_ref, g_ref, b_ref, o_ref):
    h = h_ref[...]
    hb = h.astype(BF16)
    hid = jnp.dot(hb, w1_ref[...], preferred_element_type=F32)
    hid = jnp.square(jnp.maximum(hid, 0.0)).astype(BF16)
    ff = jnp.dot(hid, w2_ref[...], preferred_element_type=F32)
    gate = jax.nn.sigmoid(jnp.dot(hb, wg_ref[...], preferred_element_type=F32))
    ple = jnp.dot(p_ref[...].astype(BF16), wp_ref[...], preferred_element_type=F32) * gate
    o_ref[...] = _layer_norm(alpha * h + ff + ple, g_ref[...], b_ref[...])


def _ffn(h, p, w1, w2, wp, wg, g, b, alpha):
    t, d = h.shape
    tm = ROW_TILE
    row = lambda w: pl.BlockSpec((tm, w), lambda i: (i, 0))
    return pl.pallas_call(
        functools.partial(_ffn_kernel, alpha),
        grid=(t // tm,),
        in_specs=[row(d), row(p.shape[1]), _const_spec(w1.shape), _const_spec(w2.shape), _const_spec(wp.shape),
                  _const_spec(wg.shape), _const_spec(g.shape), _const_spec(b.shape)],
        out_specs=row(d),
        out_shape=jax.ShapeDtypeStruct((t, d), F32),
        compiler_params=_params(1),
        name="ffn_ple",
    )(h, p, w1, w2, wp, wg, g, b)


def _row(v):
    return v.reshape(1, -1).astype(F32)


def _per_head_lanes(v):
    return jnp.repeat(v.astype(F32), SSD_HEAD_DIM).reshape(1, SSD_INNER)


def kernel(x, p, ln_in_g, ln_in_b, w_in, pool_w, pool_scale, lam_q1, lam_k1, lam_q2, lam_k2, diff_norm_g, conv_w, conv_b, dt_bias, a_log, d_skip, ssd_norm_g, w_out, ln1_g, ln1_b, w_ff1, w_ff2, w_ple, w_ple_gate, ln2_g, ln2_b):
    batch, seq, d_model = x.shape
    depth = w_in.shape[0]
    t = batch * seq
    alpha = (2 * depth) ** 0.25
    h = x.reshape(t, d_model)
    p2 = p.reshape(depth, t, p.shape[-1])
    for i in range(depth):
        lambda_init = 0.8 - 0.6 * math.exp(-0.3 * i)
        w = w_in[i]
        w_aug = jnp.concatenate([w[:, :_OFF_DT], jnp.repeat(w[:, _OFF_DT:], SSD_HEAD_DIM, axis=1)], axis=1).astype(BF16)
        outs = _in_projection(h, _row(ln_in_g), _row(ln_in_b), w_aug, apply_ln=(i == 0))
        if i == 0:
            h, outs = outs[0], outs[1:]
        u_pool, q, k, vt, z, xbc, dtf = outs

        lamv = jnp.zeros((4, LANES), F32).at[:, :DIFF_QK_DIM].set(
            jnp.stack([lam_q1[i], lam_k1[i], lam_q2[i], lam_k2[i]]).astype(F32))
        ng2 = jnp.tile(diff_norm_g[i].astype(F32), LANES // DIFF_V_DIM).reshape(1, LANES)
        o_diff = _diff_attention(q, k, vt, lamv, ng2, batch, seq, lambda_init)

        o_ssd = _ssd_mixer(xbc, z, dtf, conv_w[i].astype(F32), _row(conv_b[i]), _per_head_lanes(dt_bias[i]),
                           _per_head_lanes(a_log[i]), _per_head_lanes(d_skip[i]), _row(ssd_norm_g[i]), batch, seq)

        pool_bd = jax.scipy.linalg.block_diag(*[pool_w[i, g] for g in range(len(POOL_WINDOWS))]).astype(BF16)
        h = _out_projection(u_pool, o_diff, o_ssd, h, pool_bd, _row(pool_scale[i]), w_out[i].astype(BF16),
                            _row(ln1_g[i]), _row(ln1_b[i]), seq, alpha)
        h = _ffn(h, p2[i], w_ff1[i].astype(BF16), w_ff2[i].astype(BF16), w_ple[i].astype(BF16),
                 w_ple_gate[i].astype(BF16), _row(ln2_g[i]), _row(ln2_b[i]), alpha)
    return h.reshape(batch, seq, d_model)
```

```python
import functools
import math

import jax
import jax.numpy as jnp
from jax import lax
from jax.experimental import pallas as pl
from jax.experimental.pallas import tpu as pltpu

F32 = jnp.float32
BF16 = jnp.bfloat16

POOL_WINDOWS = (2, 4, 8, 16)
POOL_GROUP = 64
POOL_W = POOL_GROUP * len(POOL_WINDOWS)
DIFF_HEADS = 4
DIFF_QK_DIM = 32
DIFF_V_DIM = 2 * DIFF_QK_DIM
DIFF_W = DIFF_HEADS * DIFF_V_DIM
SSD_INNER = 512
SSD_HEAD_DIM = 64
SSD_HEADS = SSD_INNER // SSD_HEAD_DIM
SSD_GROUPS = 2
SSD_HEADS_PER_GROUP = SSD_HEADS // SSD_GROUPS
SSD_STATE = 128
CONV_K = 4
CONV_CH = SSD_INNER + 2 * SSD_GROUPS * SSD_STATE
CHUNK = 128
LN_EPS = 1e-5
LOG2E = 1.4426950408889634

LANES = 128
SUBLANES = 8
VMEM_LIMIT = 56 * 1024 * 1024

ROW_TILE = 512
ATT_TQ = 256
ATT_TK = 256
ATT_SUM_ROWS = 16
POOL_HALO = 16

_OFF_POOL = 0
_OFF_Q = _OFF_POOL + POOL_W
_OFF_K = _OFF_Q + DIFF_W
_OFF_V = _OFF_K + DIFF_W
_OFF_Z = _OFF_V + DIFF_W
_OFF_XBC = _OFF_Z + SSD_INNER
_OFF_DT = _OFF_XBC + CONV_CH
_AUG_W = _OFF_DT + SSD_INNER


def _layer_norm(x, g, b):
    mu = jnp.mean(x, axis=-1, keepdims=True)
    xc = x - mu
    var = jnp.mean(xc * xc, axis=-1, keepdims=True)
    return xc * lax.rsqrt(var + LN_EPS) * g + b


def _const_spec(shape):
    nd = len(shape)
    return pl.BlockSpec(shape, lambda *_: (0,) * nd, pipeline_mode=pl.Buffered(1))


def _params(n_axes):
    return pltpu.CompilerParams(dimension_semantics=("arbitrary",) * n_axes, vmem_limit_bytes=VMEM_LIMIT)


def _inproj_kernel(apply_ln, qscale, x_ref, g_ref, b_ref, w_ref, *out_refs):
    if apply_ln:
        h_ref, up_ref, q_ref, k_ref, vt_ref, z_ref, xbc_ref, dt_ref = out_refs
        x = _layer_norm(x_ref[...], g_ref[...], b_ref[...])
        h_ref[...] = x
    else:
        up_ref, q_ref, k_ref, vt_ref, z_ref, xbc_ref, dt_ref = out_refs
        x = x_ref[...]
    xb = x.astype(BF16)

    def proj(lo, width):
        return jnp.dot(xb, w_ref[:, lo:lo + width], preferred_element_type=F32)

    up_ref[...] = proj(_OFF_POOL, POOL_W)
    q_ref[...] = (proj(_OFF_Q, DIFF_W) * qscale).astype(BF16)
    k_ref[...] = proj(_OFF_K, DIFF_W).astype(BF16)
    v = proj(_OFF_V, DIFF_W)
    tk = vt_ref.shape[-1]
    for s in range(vt_ref.shape[0]):
        vt_ref[s] = v[s * tk:(s + 1) * tk, :].T.astype(BF16)
    z_ref[...] = proj(_OFF_Z, SSD_INNER)
    xbc_ref[...] = proj(_OFF_XBC, CONV_CH)
    dt_ref[...] = proj(_OFF_DT, SSD_INNER)


def _in_projection(x, ln_g, ln_b, w_aug, apply_ln):
    t, d = x.shape
    tm = ROW_TILE
    nsub = tm // ATT_TK
    qscale = DIFF_QK_DIM ** -0.5 * LOG2E
    row = lambda w: pl.BlockSpec((tm, w), lambda i: (i, 0))
    out_shape = [
        jax.ShapeDtypeStruct((t, POOL_W), F32),
        jax.ShapeDtypeStruct((t, DIFF_W), BF16),
        jax.ShapeDtypeStruct((t, DIFF_W), BF16),
        jax.ShapeDtypeStruct((t // ATT_TK, DIFF_W, ATT_TK), BF16),
        jax.ShapeDtypeStruct((t, SSD_INNER), F32),
        jax.ShapeDtypeStruct((t, CONV_CH), F32),
        jax.ShapeDtypeStruct((t, SSD_INNER), F32),
    ]
    out_specs = [row(POOL_W), row(DIFF_W), row(DIFF_W),
                 pl.BlockSpec((nsub, DIFF_W, ATT_TK), lambda i: (i, 0, 0)),
                 row(SSD_INNER), row(CONV_CH), row(SSD_INNER)]
    if apply_ln:
        out_shape = [jax.ShapeDtypeStruct((t, d), F32)] + out_shape
        out_specs = [row(d)] + out_specs
    return pl.pallas_call(
        functools.partial(_inproj_kernel, apply_ln, qscale),
        grid=(t // tm,),
        in_specs=[row(d), _const_spec((1, d)), _const_spec((1, d)), _const_spec(w_aug.shape)],
        out_specs=out_specs,
        out_shape=out_shape,
        compiler_params=_params(1),
        name="in_proj_ln" if apply_ln else "in_proj",
    )(x, ln_g, ln_b, w_aug)


def _attn_kernel(lambda_init, q_ref, k_ref, vt_ref, lamv_ref, ng_ref, o_ref, acc_ref, s0_ref, s1_ref):
    qi = pl.program_id(2)
    tq = q_ref.shape[0]
    tk = vt_ref.shape[-1]
    n_comp = LANES // DIFF_QK_DIM
    qb = q_ref[...]
    lane = lax.broadcasted_iota(jnp.int32, (1, LANES), 1)
    zero = jnp.zeros_like(qb)
    qm = [jnp.where((lane >= DIFF_QK_DIM * c) & (lane < DIFF_QK_DIM * (c + 1)), qb, zero) for c in range(n_comp)]
    acc_ref[...] = jnp.zeros_like(acc_ref)
    ones_rows = jnp.ones((ATT_SUM_ROWS, tk), BF16)

    def scores(j, s_ref, masked):
        kb = k_ref[pl.ds(pl.multiple_of(j * tk, tk), tk), :]
        if masked:
            keep = (lax.broadcasted_iota(jnp.int32, (tk, tq), 0) <= lax.broadcasted_iota(jnp.int32, (tk, tq), 1))
        maxima = []
        for c in range(n_comp):
            s = lax.dot_general(kb, qm[c], (((1,), (1,)), ((), ())), preferred_element_type=F32)
            if masked:
                s = jnp.where(keep, s, -jnp.inf)
            s_ref[c] = s
            maxima.append(jnp.max(s, axis=0, keepdims=True))
        return tuple(maxima)

    def accumulate(j, s_ref, ms, maxima):
        vb = jnp.concatenate([vt_ref[j], ones_rows], axis=0)
        new_ms = []
        for c in range(n_comp):
            m_new = jnp.maximum(ms[c], maxima[c])
            alpha = jnp.exp2(ms[c] - m_new)
            p = jnp.exp2(s_ref[c] - m_new).astype(BF16)
            acc_ref[c] = acc_ref[c] * alpha + jnp.dot(vb, p, preferred_element_type=F32)
            new_ms.append(m_new)
        return tuple(new_ms)

    ms = tuple(jnp.full((1, tq), -jnp.inf, F32) for _ in range(n_comp))
    maxima = scores(qi, s0_ref, True)

    def body(t, carry):
        ms, maxima, prev = carry

        def even():
            nxt = scores(t, s1_ref, False)
            return accumulate(prev, s0_ref, ms, maxima), nxt

        def odd():
            nxt = scores(t, s0_ref, False)
            return accumulate(prev, s1_ref, ms, maxima), nxt

        ms, maxima = lax.cond(t % 2 == 0, even, odd)
        return ms, maxima, t

    ms, maxima, prev = lax.fori_loop(0, qi, body, (ms, maxima, qi))
    lax.cond(qi % 2 == 0,
             lambda: accumulate(prev, s0_ref, ms, maxima),
             lambda: accumulate(prev, s1_ref, ms, maxima))

    lamv = lamv_ref[...]
    lam = (jnp.exp(jnp.sum(lamv[0:1] * lamv[1:2], axis=-1, keepdims=True))
           - jnp.exp(jnp.sum(lamv[2:3] * lamv[3:4], axis=-1, keepdims=True)) + lambda_init)
    row = lax.broadcasted_iota(jnp.int32, (LANES, tq), 0)
    heads = []
    for h in range(2):
        norm = [acc_ref[c, 0:LANES, :] * (1.0 / acc_ref[c, LANES:LANES + 1, :]) for c in (2 * h, 2 * h + 1)]
        heads.append(norm[0] - lam * norm[1])
    first = row < DIFF_V_DIM
    ot = jnp.where(first, heads[0], heads[1])
    sq = ot * ot
    ms0 = jnp.sum(jnp.where(first, sq, 0.0), axis=0, keepdims=True) * (1.0 / DIFF_V_DIM)
    ms1 = jnp.sum(jnp.where(first, 0.0, sq), axis=0, keepdims=True) * (1.0 / DIFF_V_DIM)
    inv = jnp.where(first, lax.rsqrt(ms0 + LN_EPS), lax.rsqrt(ms1 + LN_EPS))
    y = (ot * inv).T * ng_ref[...] * (1.0 - lambda_init)
    o_ref[...] = y


def _diff_attention(q, k, vt, lamv, ng2, batch, seq, lambda_init):
    t = q.shape[0]
    tq, tk = ATT_TQ, ATT_TK
    nq = seq // tq
    nkb = seq // tk
    n_pairs = DIFF_W // LANES
    n_comp = LANES // DIFF_QK_DIM
    assert tq == tk, "the diagonal block must be square"
    return pl.pallas_call(
        functools.partial(_attn_kernel, lambda_init),
        grid=(batch, n_pairs, nq),
        in_specs=[
            pl.BlockSpec((tq, LANES), lambda b, hp, i: (b * nq + i, hp)),
            pl.BlockSpec((seq, LANES), lambda b, hp, i: (b, hp)),
            pl.BlockSpec((nkb, LANES, tk), lambda b, hp, i: (b, hp, 0)),
            _const_spec(lamv.shape),
            _const_spec(ng2.shape),
        ],
        out_specs=pl.BlockSpec((tq, LANES), lambda b, hp, i: (b * nq + i, hp)),
        out_shape=jax.ShapeDtypeStruct((t, DIFF_W), F32),
        scratch_shapes=[pltpu.VMEM((n_comp, LANES + ATT_SUM_ROWS, tq), F32),
                        pltpu.VMEM((n_comp, tk, tq), F32), pltpu.VMEM((n_comp, tk, tq), F32)],
        compiler_params=_params(3),
        name="diff_attention",
    )(q, k, vt, lamv, ng2)


def _split3(x):
    hi = x.astype(BF16)
    r = x - hi.astype(F32)
    mid = r.astype(BF16)
    lo = (r - mid.astype(F32)).astype(BF16)
    return hi, mid, lo


def _ssd_kernel(xbc_ref, z_ref, dt_ref, cw_ref, cb_ref, dtb_ref, alog_ref, dsk_ref, ng_ref, o_ref,
                ext_ref, state_ref):
    c = pl.program_id(1)
    lc = xbc_ref.shape[0]
    g_w = SSD_HEADS_PER_GROUP * SSD_HEAD_DIM
    n = SSD_STATE

    @pl.when(c == 0)
    def _():
        state_ref[...] = jnp.zeros_like(state_ref)
        ext_ref[0:SUBLANES, :] = jnp.zeros((SUBLANES, CONV_CH), F32)

    xin = xbc_ref[...]
    ext_ref[SUBLANES:, :] = xin
    cw = cw_ref[...]
    conv = cb_ref[...] + cw[CONV_K - 1:CONV_K] * xin
    for j in range(CONV_K - 1):
        off = SUBLANES - (CONV_K - 1) + j
        conv = conv + cw[j:j + 1] * ext_ref[off:off + lc, :]
    ext_ref[0:SUBLANES, :] = xin[lc - SUBLANES:, :]
    act = conv * jax.nn.sigmoid(conv)
    xs = act[:, :SSD_INNER]
    bm = act[:, SSD_INNER:SSD_INNER + SSD_GROUPS * n]
    cm = act[:, SSD_INNER + SSD_GROUPS * n:]

    dt_in = dt_ref[...] + dtb_ref[...]
    dt = jnp.maximum(dt_in, 0.0) + jnp.log(1.0 + jnp.exp(-jnp.abs(dt_in)))
    a_neg = -jnp.exp(alog_ref[...])
    dta = dt * a_neg
    ri = lax.broadcasted_iota(jnp.int32, (lc, lc), 0)
    ci = lax.broadcasted_iota(jnp.int32, (lc, lc), 1)
    causal = ri >= ci
    tri = jnp.where(causal, 1.0, 0.0).astype(BF16)
    acs = None
    for part in _split3(dta):
        term = jnp.dot(tri, part, preferred_element_type=F32)
        acs = term if acs is None else acs + term
    acs_last = acs[lc - 1:lc, :]
    xdt = xs * dt
    decayed = (xdt * jnp.exp(acs_last - acs)).astype(BF16)
    xdt_b = xdt.astype(BF16)
    exp_acs = jnp.exp(acs)
    chunk_decay = jnp.exp(acs_last)

    lane_g = lax.broadcasted_iota(jnp.int32, (1, g_w), 1)
    lane128 = lax.broadcasted_iota(jnp.int32, (1, LANES), 1)
    y_parts = []
    for g in range(SSD_GROUPS):
        bm_g = bm[:, g * n:(g + 1) * n]
        cm_g = cm[:, g * n:(g + 1) * n].astype(BF16)
        cb = lax.dot_general(cm_g, bm_g.astype(BF16), (((1,), (1,)), ((), ())), preferred_element_type=F32)
        xdt_g = xdt_b[:, g * g_w:(g + 1) * g_w]
        y_g = None
        for pair in range(SSD_HEADS_PER_GROUP // 2):
            lo = g * g_w + pair * LANES
            a_pair = acs[:, lo:lo + LANES]
            a_swap = pltpu.roll(a_pair, DIFF_V_DIM, 1)
            a_t = a_pair.T
            for sub in range(2):
                r = 2 * pair + sub
                own = (lane128 < SSD_HEAD_DIM) if sub == 0 else (lane128 >= SSD_HEAD_DIM)
                col = jnp.where(own, a_pair, a_swap)
                rowv = a_t[sub * SSD_HEAD_DIM:sub * SSD_HEAD_DIM + 1, :]
                lmat = jnp.where(causal, jnp.exp(col - rowv), 0.0)
                w = (cb * lmat).astype(BF16)
                sel = (lane_g >= r * SSD_HEAD_DIM) & (lane_g < (r + 1) * SSD_HEAD_DIM)
                contrib = jnp.dot(w, jnp.where(sel, xdt_g, jnp.zeros_like(xdt_g)), preferred_element_type=F32)
                y_g = contrib if y_g is None else y_g + contrib
        prev = state_ref[g]
        y_off = jnp.dot(cm_g, prev.astype(BF16), preferred_element_type=F32) * exp_acs[:, g * g_w:(g + 1) * g_w]
        st = jnp.dot(bm_g.T.astype(BF16), decayed[:, g * g_w:(g + 1) * g_w], preferred_element_type=F32)
        state_ref[g] = prev * chunk_decay[:, g * g_w:(g + 1) * g_w] + st
        y_parts.append(y_g + y_off)
    y = jnp.concatenate(y_parts, axis=1) + xs * dsk_ref[...]
    zz = z_ref[...]
    y = y * (zz * jax.nn.sigmoid(zz))
    y = y * lax.rsqrt(jnp.mean(y * y, axis=-1, keepdims=True) + LN_EPS)
    o_ref[...] = y * ng_ref[...]


def _ssd_mixer(xbc, z, dtf, conv_w, conv_b, dtb_f, alog_f, dsk_f, norm_g, batch, seq):
    t = xbc.shape[0]
    nc = seq // CHUNK
    row = lambda w: pl.BlockSpec((CHUNK, w), lambda b, c: (b * nc + c, 0))
    return pl.pallas_call(
        _ssd_kernel,
        grid=(batch, nc),
        in_specs=[row(CONV_CH), row(SSD_INNER), row(SSD_INNER),
                  _const_spec(conv_w.shape), _const_spec(conv_b.shape), _const_spec(dtb_f.shape),
                  _const_spec(alog_f.shape), _const_spec(dsk_f.shape), _const_spec(norm_g.shape)],
        out_specs=row(SSD_INNER),
        out_shape=jax.ShapeDtypeStruct((t, SSD_INNER), F32),
        scratch_shapes=[pltpu.VMEM((SUBLANES + CHUNK, CONV_CH), F32),
                        pltpu.VMEM((SSD_GROUPS, SSD_STATE, SSD_HEADS_PER_GROUP * SSD_HEAD_DIM), F32)],
        compiler_params=_params(2),
        name="ssd_mixer",
    )(xbc, z, dtf, conv_w, conv_b, dtb_f, alog_f, dsk_f, norm_g)


def _outproj_kernel(alpha, blocks_per_seq, u_ref, halo_ref, od_ref, os_ref, h_ref, pw_ref, ps_ref, wo_ref,
                    g_ref, b_ref, o_ref, lvl_ref):
    i = pl.program_id(0)
    tm = u_ref.shape[0]
    base = 2 * POOL_HALO
    seq_blk = i % blocks_per_seq
    u = u_ref[...]
    halo = halo_ref[...]
    lvl_ref[:, 0:POOL_HALO, :] = jnp.zeros((len(POOL_WINDOWS), POOL_HALO, POOL_W), F32)
    lvl_ref[0, POOL_HALO:base, :] = jnp.where(seq_blk == 0, jnp.zeros_like(halo), halo)
    lvl_ref[0, base:, :] = u
    n_rows = tm + POOL_HALO
    sums = []
    for lv, shift in enumerate((1, 2, 4, 8)):
        cur = lvl_ref[lv, POOL_HALO:, :] + lvl_ref[lv, POOL_HALO - shift:POOL_HALO - shift + n_rows, :]
        if lv + 1 < len(POOL_WINDOWS):
            lvl_ref[lv + 1, POOL_HALO:, :] = cur
        sums.append(cur[POOL_HALO:, :])
    pos = (seq_blk * tm + lax.broadcasted_iota(jnp.int32, (tm, 1), 0) + 1).astype(F32)
    lane = lax.broadcasted_iota(jnp.int32, (1, POOL_W), 1)
    pooled = jnp.zeros((tm, POOL_W), F32)
    for gi, w in enumerate(POOL_WINDOWS):
        mean = sums[gi] / jnp.minimum(pos, float(w))
        pooled = jnp.where((lane >= gi * POOL_GROUP) & (lane < (gi + 1) * POOL_GROUP), mean, pooled)
    pooled = pooled - u
    o_pool = jnp.dot(pooled.astype(BF16), pw_ref[...], preferred_element_type=F32) * ps_ref[...]
    cat = jnp.concatenate([o_pool.astype(BF16), od_ref[...].astype(BF16), os_ref[...].astype(BF16)], axis=1)
    mix = jnp.dot(cat, wo_ref[...], preferred_element_type=F32)
    o_ref[...] = _layer_norm(alpha * h_ref[...] + mix, g_ref[...], b_ref[...])


def _out_projection(u_pool, o_diff, o_ssd, h, pool_bd, pool_scale, w_out, g, b, seq, alpha):
    t, d = h.shape
    tm = ROW_TILE
    blocks_per_seq = seq // tm
    halo_per_tile = tm // POOL_HALO
    row = lambda w: pl.BlockSpec((tm, w), lambda i: (i, 0))
    return pl.pallas_call(
        functools.partial(_outproj_kernel, alpha, blocks_per_seq),
        grid=(t // tm,),
        in_specs=[row(POOL_W),
                  pl.BlockSpec((POOL_HALO, POOL_W), lambda i: (jnp.maximum(i * halo_per_tile - 1, 0), 0)),
                  row(DIFF_W), row(SSD_INNER), row(d),
                  _const_spec(pool_bd.shape), _const_spec(pool_scale.shape), _const_spec(w_out.shape),
                  _const_spec(g.shape), _const_spec(b.shape)],
        out_specs=row(d),
        out_shape=jax.ShapeDtypeStruct((t, d), F32),
        scratch_shapes=[pltpu.VMEM((len(POOL_WINDOWS), tm + 2 * POOL_HALO, POOL_W), F32)],
        compiler_params=_params(1),
        name="out_proj",
    )(u_pool, u_pool, o_diff, o_ssd, h, pool_bd, pool_scale, w_out, g, b)


def _ffn_kernel(alpha, h_ref, p_ref, w1_ref, w2_ref, wp_ref, wg_ref, g_ref, b_ref, o_ref):
    h = h_ref[...]
    hb = h.astype(BF16)
    hid = jnp.dot(hb, w1_ref[...], preferred_element_type=F32)
    hid = jnp.square(jnp.maximum(hid, 0.0)).astype(BF16)
    ff = jnp.dot(hid, w2_ref[...], preferred_element_type=F32)
    gate = jax.nn.sigmoid(jnp.dot(hb, wg_ref[...], preferred_element_type=F32))
    ple = jnp.dot(p_ref[...].astype(BF16), wp_ref[...], preferred_element_type=F32) * gate
    o_ref[...] = _layer_norm(alpha * h + ff + ple, g_ref[...], b_ref[...])


def _ffn(h, p, w1, w2, wp, wg, g, b, alpha):
    t, d = h.shape
    tm = ROW_TILE
    row = lambda w: pl.BlockSpec((tm, w), lambda i: (i, 0))
    return pl.pallas_call(
        functools.partial(_ffn_kernel, alpha),
        grid=(t // tm,),
        in_specs=[row(d), row(p.shape[1]), _const_spec(w1.shape), _const_spec(w2.shape), _const_spec(wp.shape),
                  _const_spec(wg.shape), _const_spec(g.shape), _const_spec(b.shape)],
        out_specs=row(d),
        out_shape=jax.ShapeDtypeStruct((t, d), F32),
        compiler_params=_params(1),
        name="ffn_ple",
    )(h, p, w1, w2, wp, wg, g, b)


def _row(v):
    return v.reshape(1, -1).astype(F32)


def _per_head_lanes(v):
    return jnp.repeat(v.astype(F32), SSD_HEAD_DIM).reshape(1, SSD_INNER)


def kernel(x, p, ln_in_g, ln_in_b, w_in, pool_w, pool_scale, lam_q1, lam_k1, lam_q2, lam_k2, diff_norm_g, conv_w, conv_b, dt_bias, a_log, d_skip, ssd_norm_g, w_out, ln1_g, ln1_b, w_ff1, w_ff2, w_ple, w_ple_gate, ln2_g, ln2_b):
    batch, seq, d_model = x.shape
    depth = w_in.shape[0]
    t = batch * seq
    alpha = (2 * depth) ** 0.25
    h = x.reshape(t, d_model)
    p2 = p.reshape(depth, t, p.shape[-1])
    for i in range(depth):
        lambda_init = 0.8 - 0.6 * math.exp(-0.3 * i)
        w = w_in[i]
        w_aug = jnp.concatenate([w[:, :_OFF_DT], jnp.repeat(w[:, _OFF_DT:], SSD_HEAD_DIM, axis=1)], axis=1).astype(BF16)
        outs = _in_projection(h, _row(ln_in_g), _row(ln_in_b), w_aug, apply_ln=(i == 0))
        if i == 0:
            h, outs = outs[0], outs[1:]
        u_pool, q, k, vt, z, xbc, dtf = outs

        lamv = jnp.zeros((4, LANES), F32).at[:, :DIFF_QK_DIM].set(
            jnp.stack([lam_q1[i], lam_k1[i], lam_q2[i], lam_k2[i]]).astype(F32))
        ng2 = jnp.tile(diff_norm_g[i].astype(F32), LANES // DIFF_V_DIM).reshape(1, LANES)
        o_diff = _diff_attention(q, k, vt, lamv, ng2, batch, seq, lambda_init)

        o_ssd = _ssd_mixer(xbc, z, dtf, conv_w[i].astype(F32), _row(conv_b[i]), _per_head_lanes(dt_bias[i]),
                           _per_head_lanes(a_log[i]), _per_head_lanes(d_skip[i]), _row(ssd_norm_g[i]), batch, seq)

        pool_bd = jax.scipy.linalg.block_diag(*[pool_w[i, g] for g in range(len(POOL_WINDOWS))]).astype(BF16)
        h = _out_projection(u_pool, o_diff, o_ssd, h, pool_bd, _row(pool_scale[i]), w_out[i].astype(BF16),
                            _row(ln1_g[i]), _row(ln1_b[i]), seq, alpha)
        h = _ffn(h, p2[i], w_ff1[i].astype(BF16), w_ff2[i].astype(BF16), w_ple[i].astype(BF16),
                 w_ple_gate[i].astype(BF16), _row(ln2_g[i]), _row(ln2_b[i]), alpha)
    return h.reshape(batch, seq, d_model)
```

```python
import functools
import math

import jax
import jax.numpy as jnp
from jax import lax
from jax.experimental import pallas as pl
from jax.experimental.pallas import tpu as pltpu

F32 = jnp.float32
BF16 = jnp.bfloat16

POOL_WINDOWS = (2, 4, 8, 16)
POOL_GROUP = 64
POOL_W = POOL_GROUP * len(POOL_WINDOWS)
DIFF_HEADS = 4
DIFF_QK_DIM = 32
DIFF_V_DIM = 2 * DIFF_QK_DIM
DIFF_W = DIFF_HEADS * DIFF_V_DIM
SSD_INNER = 512
SSD_HEAD_DIM = 64
SSD_HEADS = SSD_INNER // SSD_HEAD_DIM
SSD_GROUPS = 2
SSD_HEADS_PER_GROUP = SSD_HEADS // SSD_GROUPS
SSD_STATE = 128
CONV_K = 4
CONV_CH = SSD_INNER + 2 * SSD_GROUPS * SSD_STATE
CHUNK = 128
LN_EPS = 1e-5
LOG2E = 1.4426950408889634

LANES = 128
SUBLANES = 8
VMEM_LIMIT = 56 * 1024 * 1024

ROW_TILE = 512
ATT_TQ = 256
ATT_TK = 256
ATT_SUM_ROWS = 16
POOL_HALO = 16

_OFF_POOL = 0
_OFF_Q = _OFF_POOL + POOL_W
_OFF_K = _OFF_Q + DIFF_W
_OFF_V = _OFF_K + DIFF_W
_OFF_Z = _OFF_V + DIFF_W
_OFF_XBC = _OFF_Z + SSD_INNER
_OFF_DT = _OFF_XBC + CONV_CH
_AUG_W = _OFF_DT + SSD_INNER


def _layer_norm(x, g, b):
    mu = jnp.mean(x, axis=-1, keepdims=True)
    xc = x - mu
    var = jnp.mean(xc * xc, axis=-1, keepdims=True)
    return xc * lax.rsqrt(var + LN_EPS) * g + b


def _const_spec(shape):
    nd = len(shape)
    return pl.BlockSpec(shape, lambda *_: (0,) * nd, pipeline_mode=pl.Buffered(1))


def _params(n_axes):
    return pltpu.CompilerParams(dimension_semantics=("arbitrary",) * n_axes, vmem_limit_bytes=VMEM_LIMIT)


def _inproj_kernel(apply_ln, qscale, x_ref, g_ref, b_ref, w_ref, *out_refs):
    if apply_ln:
        h_ref, up_ref, q_ref, k_ref, vt_ref, z_ref, xbc_ref, dt_ref = out_refs
        x = _layer_norm(x_ref[...], g_ref[...], b_ref[...])
        h_ref[...] = x
    else:
        up_ref, q_ref, k_ref, vt_ref, z_ref, xbc_ref, dt_ref = out_refs
        x = x_ref[...]
    xb = x.astype(BF16)

    def proj(lo, width):
        return jnp.dot(xb, w_ref[:, lo:lo + width], preferred_element_type=F32)

    up_ref[...] = proj(_OFF_POOL, POOL_W)
    q_ref[...] = (proj(_OFF_Q, DIFF_W) * qscale).astype(BF16)
    k_ref[...] = proj(_OFF_K, DIFF_W).astype(BF16)
    v = proj(_OFF_V, DIFF_W)
    tk = vt_ref.shape[-1]
    for s in range(vt_ref.shape[0]):
        vt_ref[s] = v[s * tk:(s + 1) * tk, :].T.astype(BF16)
    z_ref[...] = proj(_OFF_Z, SSD_INNER)
    xbc_ref[...] = proj(_OFF_XBC, CONV_CH)
    dt_ref[...] = proj(_OFF_DT, SSD_INNER)


def _in_projection(x, ln_g, ln_b, w_aug, apply_ln):
    t, d = x.shape
    tm = ROW_TILE
    nsub = tm // ATT_TK
    qscale = DIFF_QK_DIM ** -0.5 * LOG2E
    row = lambda w: pl.BlockSpec((tm, w), lambda i: (i, 0))
    out_shape = [
        jax.ShapeDtypeStruct((t, POOL_W), F32),
        jax.ShapeDtypeStruct((t, DIFF_W), BF16),
        jax.ShapeDtypeStruct((t, DIFF_W), BF16),
        jax.ShapeDtypeStruct((t // ATT_TK, DIFF_W, ATT_TK), BF16),
        jax.ShapeDtypeStruct((t, SSD_INNER), F32),
        jax.ShapeDtypeStruct((t, CONV_CH), F32),
        jax.ShapeDtypeStruct((t, SSD_INNER), F32),
    ]
    out_specs = [row(POOL_W), row(DIFF_W), row(DIFF_W),
                 pl.BlockSpec((nsub, DIFF_W, ATT_TK), lambda i: (i, 0, 0)),
                 row(SSD_INNER), row(CONV_CH), row(SSD_INNER)]
    if apply_ln:
        out_shape = [jax.ShapeDtypeStruct((t, d), F32)] + out_shape
        out_specs = [row(d)] + out_specs
    return pl.pallas_call(
        functools.partial(_inproj_kernel, apply_ln, qscale),
        grid=(t // tm,),
        in_specs=[row(d), _const_spec((1, d)), _const_spec((1, d)), _const_spec(w_aug.shape)],
        out_specs=out_specs,
        out_shape=out_shape,
        compiler_params=_params(1),
        name="in_proj_ln" if apply_ln else "in_proj",
    )(x, ln_g, ln_b, w_aug)


def _attn_kernel(lambda_init, q_ref, k_ref, vt_ref, lamv_ref, ng_ref, o_ref, acc_ref, s0_ref, s1_ref):
    qi = pl.program_id(2)
    tq = q_ref.shape[0]
    tk = vt_ref.shape[-1]
    n_comp = LANES // DIFF_QK_DIM
    qb = q_ref[...]
    lane = lax.broadcasted_iota(jnp.int32, (1, LANES), 1)
    zero = jnp.zeros_like(qb)
    q4 = jnp.concatenate(
        [jnp.where((lane >= DIFF_QK_DIM * c) & (lane < DIFF_QK_DIM * (c + 1)), qb, zero) for c in range(n_comp)],
        axis=0)
    wide = n_comp * tq
    acc_ref[...] = jnp.zeros_like(acc_ref)
    ones_rows = jnp.ones((ATT_SUM_ROWS, tk), BF16)

    def scores(j, s_ref, masked):
        kb = k_ref[pl.ds(pl.multiple_of(j * tk, tk), tk), :]
        s = lax.dot_general(kb, q4, (((1,), (1,)), ((), ())), preferred_element_type=F32)
        if masked:
            key_i = lax.broadcasted_iota(jnp.int32, (tk, wide), 0)
            query_i = lax.broadcasted_iota(jnp.int32, (tk, wide), 1) & (tq - 1)
            s = jnp.where(key_i <= query_i, s, -jnp.inf)
        s_ref[...] = s
        return jnp.max(s, axis=0, keepdims=True)

    def accumulate(j, s_ref, ms, maxima):
        vb = jnp.concatenate([vt_ref[j], ones_rows], axis=0)
        m_new = jnp.maximum(ms, maxima)
        alpha = jnp.exp2(ms - m_new)
        p = jnp.exp2(s_ref[...] - m_new).astype(BF16)
        acc_ref[...] = acc_ref[...] * alpha + jnp.dot(vb, p, preferred_element_type=F32)
        return m_new

    ms = jnp.full((1, wide), -jnp.inf, F32)
    maxima = scores(qi, s0_ref, True)

    def step(t, prev, s_write, s_read, ms, maxima):
        nxt = scores(t, s_write, False)
        return accumulate(prev, s_read, ms, maxima), nxt

    def pair(u, carry):
        ms, maxima, prev = carry
        ms, maxima = step(2 * u, prev, s1_ref, s0_ref, ms, maxima)
        ms, maxima = step(2 * u + 1, 2 * u, s0_ref, s1_ref, ms, maxima)
        return ms, maxima, 2 * u + 1

    def quad(v, carry):
        return pair(2 * v + 1, pair(2 * v, carry))

    n_pairs = qi // 2
    carry = lax.fori_loop(0, n_pairs // 2, quad, (ms, maxima, qi))
    ms, maxima, prev = lax.cond(n_pairs % 2 == 1, lambda c: pair(n_pairs - 1, c), lambda c: c, carry)

    def odd_tail():
        ms1, maxima1 = step(qi - 1, prev, s1_ref, s0_ref, ms, maxima)
        accumulate(qi - 1, s1_ref, ms1, maxima1)

    lax.cond(qi % 2 == 1, odd_tail, lambda: (accumulate(prev, s0_ref, ms, maxima), None)[1])

    lamv = lamv_ref[...]
    lam = (jnp.exp(jnp.sum(lamv[0:1] * lamv[1:2], axis=-1, keepdims=True))
           - jnp.exp(jnp.sum(lamv[2:3] * lamv[3:4], axis=-1, keepdims=True)) + lambda_init)
    row = lax.broadcasted_iota(jnp.int32, (LANES, tq), 0)
    heads = []
    for h in range(2):
        norm = [acc_ref[0:LANES, c * tq:(c + 1) * tq] * (1.0 / acc_ref[LANES:LANES + 1, c * tq:(c + 1) * tq])
                for c in (2 * h, 2 * h + 1)]
        heads.append(norm[0] - lam * norm[1])
    first = row < DIFF_V_DIM
    ot = jnp.where(first, heads[0], heads[1])
    sq = ot * ot
    ms0 = jnp.sum(jnp.where(first, sq, 0.0), axis=0, keepdims=True) * (1.0 / DIFF_V_DIM)
    ms1 = jnp.sum(jnp.where(first, 0.0, sq), axis=0, keepdims=True) * (1.0 / DIFF_V_DIM)
    inv = jnp.where(first, lax.rsqrt(ms0 + LN_EPS), lax.rsqrt(ms1 + LN_EPS))
    y = (ot * inv).T * ng_ref[...] * (1.0 - lambda_init)
    o_ref[...] = y


def _diff_attention(q, k, vt, lamv, ng2, batch, seq, lambda_init):
    t = q.shape[0]
    tq, tk = ATT_TQ, ATT_TK
    nq = seq // tq
    nkb = seq // tk
    n_pairs = DIFF_W // LANES
    n_comp = LANES // DIFF_QK_DIM
    assert tq == tk, "the diagonal block must be square"
    return pl.pallas_call(
        functools.partial(_attn_kernel, lambda_init),
        grid=(batch, n_pairs, nq),
        in_specs=[
            pl.BlockSpec((tq, LANES), lambda b, hp, i: (b * nq + i, hp)),
            pl.BlockSpec((seq, LANES), lambda b, hp, i: (b, hp)),
            pl.BlockSpec((nkb, LANES, tk), lambda b, hp, i: (b, hp, 0)),
            _const_spec(lamv.shape),
            _const_spec(ng2.shape),
        ],
        out_specs=pl.BlockSpec((tq, LANES), lambda b, hp, i: (b * nq + i, hp)),
        out_shape=jax.ShapeDtypeStruct((t, DIFF_W), F32),
        scratch_shapes=[pltpu.VMEM((LANES + ATT_SUM_ROWS, n_comp * tq), F32),
                        pltpu.VMEM((tk, n_comp * tq), F32), pltpu.VMEM((tk, n_comp * tq), F32)],
        compiler_params=_params(3),
        name="diff_attention",
    )(q, k, vt, lamv, ng2)


def _split3(x):
    hi = x.astype(BF16)
    r = x - hi.astype(F32)
    mid = r.astype(BF16)
    lo = (r - mid.astype(F32)).astype(BF16)
    return hi, mid, lo


def _ssd_kernel(xbc_ref, z_ref, dt_ref, cw_ref, cb_ref, dtb_ref, alog_ref, dsk_ref, ng_ref, o_ref,
                ext_ref, state_ref):
    c = pl.program_id(1)
    lc = xbc_ref.shape[0]
    g_w = SSD_HEADS_PER_GROUP * SSD_HEAD_DIM
    n = SSD_STATE

    @pl.when(c == 0)
    def _():
        state_ref[...] = jnp.zeros_like(state_ref)
        ext_ref[0:SUBLANES, :] = jnp.zeros((SUBLANES, CONV_CH), F32)

    xin = xbc_ref[...]
    ext_ref[SUBLANES:, :] = xin
    cw = cw_ref[...]
    conv = cb_ref[...] + cw[CONV_K - 1:CONV_K] * xin
    for j in range(CONV_K - 1):
        off = SUBLANES - (CONV_K - 1) + j
        conv = conv + cw[j:j + 1] * ext_ref[off:off + lc, :]
    ext_ref[0:SUBLANES, :] = xin[lc - SUBLANES:, :]
    act = conv * jax.nn.sigmoid(conv)
    xs = act[:, :SSD_INNER]
    bm = act[:, SSD_INNER:SSD_INNER + SSD_GROUPS * n]
    cm = act[:, SSD_INNER + SSD_GROUPS * n:]

    dt_in = dt_ref[...] + dtb_ref[...]
    dt = jnp.maximum(dt_in, 0.0) + jnp.log(1.0 + jnp.exp(-jnp.abs(dt_in)))
    a_neg = -jnp.exp(alog_ref[...])
    dta = dt * a_neg
    ri = lax.broadcasted_iota(jnp.int32, (lc, lc), 0)
    ci = lax.broadcasted_iota(jnp.int32, (lc, lc), 1)
    causal = ri >= ci
    tri = jnp.where(causal, 1.0, 0.0).astype(BF16)
    acs = None
    for part in _split3(dta):
        term = jnp.dot(tri, part, preferred_element_type=F32)
        acs = term if acs is None else acs + term
    acs_last = acs[lc - 1:lc, :]
    xdt = xs * dt
    decayed = (xdt * jnp.exp(acs_last - acs)).astype(BF16)
    xdt_b = xdt.astype(BF16)
    exp_acs = jnp.exp(acs)
    chunk_decay = jnp.exp(acs_last)

    lane_g = lax.broadcasted_iota(jnp.int32, (1, g_w), 1)
    lane128 = lax.broadcasted_iota(jnp.int32, (1, LANES), 1)
    y_parts = []
    for g in range(SSD_GROUPS):
        bm_g = bm[:, g * n:(g + 1) * n]
        cm_g = cm[:, g * n:(g + 1) * n].astype(BF16)
        cb = lax.dot_general(cm_g, bm_g.astype(BF16), (((1,), (1,)), ((), ())), preferred_element_type=F32)
        xdt_g = xdt_b[:, g * g_w:(g + 1) * g_w]
        y_g = None
        for pair in range(SSD_HEADS_PER_GROUP // 2):
            lo = g * g_w + pair * LANES
            a_pair = acs[:, lo:lo + LANES]
            a_swap = pltpu.roll(a_pair, DIFF_V_DIM, 1)
            a_t = a_pair.T
            for sub in range(2):
                r = 2 * pair + sub
                own = (lane128 < SSD_HEAD_DIM) if sub == 0 else (lane128 >= SSD_HEAD_DIM)
                col = jnp.where(own, a_pair, a_swap)
                rowv = a_t[sub * SSD_HEAD_DIM:sub * SSD_HEAD_DIM + 1, :]
                lmat = jnp.where(causal, jnp.exp(col - rowv), 0.0)
                w = (cb * lmat).astype(BF16)
                sel = (lane_g >= r * SSD_HEAD_DIM) & (lane_g < (r + 1) * SSD_HEAD_DIM)
                contrib = jnp.dot(w, jnp.where(sel, xdt_g, jnp.zeros_like(xdt_g)), preferred_element_type=F32)
                y_g = contrib if y_g is None else y_g + contrib
        prev = state_ref[g]
        y_off = jnp.dot(cm_g, prev.astype(BF16), preferred_element_type=F32) * exp_acs[:, g * g_w:(g + 1) * g_w]
        st = jnp.dot(bm_g.T.astype(BF16), decayed[:, g * g_w:(g + 1) * g_w], preferred_element_type=F32)
        state_ref[g] = prev * chunk_decay[:, g * g_w:(g + 1) * g_w] + st
        y_parts.append(y_g + y_off)
    y = jnp.concatenate(y_parts, axis=1) + xs * dsk_ref[...]
    zz = z_ref[...]
    y = y * (zz * jax.nn.sigmoid(zz))
    y = y * lax.rsqrt(jnp.mean(y * y, axis=-1, keepdims=True) + LN_EPS)
    o_ref[...] = y * ng_ref[...]


def _ssd_mixer(xbc, z, dtf, conv_w, conv_b, dtb_f, alog_f, dsk_f, norm_g, batch, seq):
    t = xbc.shape[0]
    nc = seq // CHUNK
    row = lambda w: pl.BlockSpec((CHUNK, w), lambda b, c: (b * nc + c, 0))
    return pl.pallas_call(
        _ssd_kernel,
        grid=(batch, nc),
        in_specs=[row(CONV_CH), row(SSD_INNER), row(SSD_INNER),
                  _const_spec(conv_w.shape), _const_spec(conv_b.shape), _const_spec(dtb_f.shape),
                  _const_spec(alog_f.shape), _const_spec(dsk_f.shape), _const_spec(norm_g.shape)],
        out_specs=row(SSD_INNER),
        out_shape=jax.ShapeDtypeStruct((t, SSD_INNER), F32),
        scratch_shapes=[pltpu.VMEM((SUBLANES + CHUNK, CONV_CH), F32),
                        pltpu.VMEM((SSD_GROUPS, SSD_STATE, SSD_HEADS_PER_GROUP * SSD_HEAD_DIM), F32)],
        compiler_params=_params(2),
        name="ssd_mixer",
    )(xbc, z, dtf, conv_w, conv_b, dtb_f, alog_f, dsk_f, norm_g)


def _pooled(seq_blk, u_ref, halo_ref, lvl_ref):
    tm = u_ref.shape[0]
    base = 2 * POOL_HALO
    u = u_ref[...]
    halo = halo_ref[...]
    lvl_ref[:, 0:POOL_HALO, :] = jnp.zeros((len(POOL_WINDOWS), POOL_HALO, POOL_W), F32)
    lvl_ref[0, POOL_HALO:base, :] = jnp.where(seq_blk == 0, jnp.zeros_like(halo), halo)
    lvl_ref[0, base:, :] = u
    n_rows = tm + POOL_HALO
    sums = []
    for lv, shift in enumerate((1, 2, 4, 8)):
        cur = lvl_ref[lv, POOL_HALO:, :] + lvl_ref[lv, POOL_HALO - shift:POOL_HALO - shift + n_rows, :]
        if lv + 1 < len(POOL_WINDOWS):
            lvl_ref[lv + 1, POOL_HALO:, :] = cur
        sums.append(cur[POOL_HALO:, :])
    pos = (seq_blk * tm + lax.broadcasted_iota(jnp.int32, (tm, 1), 0) + 1).astype(F32)
    lane = lax.broadcasted_iota(jnp.int32, (1, POOL_W), 1)
    pooled = jnp.zeros((tm, POOL_W), F32)
    for gi, w in enumerate(POOL_WINDOWS):
        mean = sums[gi] / jnp.minimum(pos, float(w))
        pooled = jnp.where((lane >= gi * POOL_GROUP) & (lane < (gi + 1) * POOL_GROUP), mean, pooled)
    return pooled - u


def _mix_ffn_kernel(alpha, blocks_per_seq, u_ref, halo_ref, od_ref, os_ref, h_ref, p_ref, pw_ref, ps_ref, wo_ref,
                    g1_ref, b1_ref, w1_ref, w2_ref, wp_ref, wg_ref, g2_ref, b2_ref, o_ref, lvl_ref):
    seq_blk = pl.program_id(0) % blocks_per_seq
    pooled = _pooled(seq_blk, u_ref, halo_ref, lvl_ref)
    o_pool = jnp.dot(pooled.astype(BF16), pw_ref[...], preferred_element_type=F32) * ps_ref[...]
    cat = jnp.concatenate([o_pool.astype(BF16), od_ref[...].astype(BF16), os_ref[...].astype(BF16)], axis=1)
    mix = jnp.dot(cat, wo_ref[...], preferred_element_type=F32)
    h = _layer_norm(alpha * h_ref[...] + mix, g1_ref[...], b1_ref[...])

    hb = h.astype(BF16)
    hid = jnp.dot(hb, w1_ref[...], preferred_element_type=F32)
    hid = jnp.square(jnp.maximum(hid, 0.0)).astype(BF16)
    ff = jnp.dot(hid, w2_ref[...], preferred_element_type=F32)
    gate = jax.nn.sigmoid(jnp.dot(hb, wg_ref[...], preferred_element_type=F32))
    ple = jnp.dot(p_ref[...].astype(BF16), wp_ref[...], preferred_element_type=F32) * gate
    o_ref[...] = _layer_norm(alpha * h + ff + ple, g2_ref[...], b2_ref[...])


def _mix_ffn(u_pool, o_diff, o_ssd, h, p, consts, seq, alpha):
    t, d = h.shape
    tm = ROW_TILE
    blocks_per_seq = seq // tm
    halo_per_tile = tm // POOL_HALO
    row = lambda w: pl.BlockSpec((tm, w), lambda i: (i, 0))
    return pl.pallas_call(
        functools.partial(_mix_ffn_kernel, alpha, blocks_per_seq),
        grid=(t // tm,),
        in_specs=[row(POOL_W),
                  pl.BlockSpec((POOL_HALO, POOL_W), lambda i: (jnp.maximum(i * halo_per_tile - 1, 0), 0)),
                  row(DIFF_W), row(SSD_INNER), row(d), row(p.shape[1])] + [_const_spec(c.shape) for c in consts],
        out_specs=row(d),
        out_shape=jax.ShapeDtypeStruct((t, d), F32),
        scratch_shapes=[pltpu.VMEM((len(POOL_WINDOWS), tm + 2 * POOL_HALO, POOL_W), F32)],
        compiler_params=_params(1),
        name="mix_ffn",
    )(u_pool, u_pool, o_diff, o_ssd, h, p, *consts)


def _row(v):
    return v.reshape(1, -1).astype(F32)


def _per_head_lanes(v):
    return jnp.repeat(v.astype(F32), SSD_HEAD_DIM).reshape(1, SSD_INNER)


def kernel(x, p, ln_in_g, ln_in_b, w_in, pool_w, pool_scale, lam_q1, lam_k1, lam_q2, lam_k2, diff_norm_g, conv_w, conv_b, dt_bias, a_log, d_skip, ssd_norm_g, w_out, ln1_g, ln1_b, w_ff1, w_ff2, w_ple, w_ple_gate, ln2_g, ln2_b):
    batch, seq, d_model = x.shape
    depth = w_in.shape[0]
    t = batch * seq
    alpha = (2 * depth) ** 0.25
    h = x.reshape(t, d_model)
    p2 = p.reshape(depth, t, p.shape[-1])
    for i in range(depth):
        lambda_init = 0.8 - 0.6 * math.exp(-0.3 * i)
        w = w_in[i]
        w_aug = jnp.concatenate([w[:, :_OFF_DT], jnp.repeat(w[:, _OFF_DT:], SSD_HEAD_DIM, axis=1)], axis=1).astype(BF16)
        outs = _in_projection(h, _row(ln_in_g), _row(ln_in_b), w_aug, apply_ln=(i == 0))
        if i == 0:
            h, outs = outs[0], outs[1:]
        u_pool, q, k, vt, z, xbc, dtf = outs

        lamv = jnp.zeros((4, LANES), F32).at[:, :DIFF_QK_DIM].set(
            jnp.stack([lam_q1[i], lam_k1[i], lam_q2[i], lam_k2[i]]).astype(F32))
        ng2 = jnp.tile(diff_norm_g[i].astype(F32), LANES // DIFF_V_DIM).reshape(1, LANES)
        o_diff = _diff_attention(q, k, vt, lamv, ng2, batch, seq, lambda_init)

        o_ssd = _ssd_mixer(xbc, z, dtf, conv_w[i].astype(F32), _row(conv_b[i]), _per_head_lanes(dt_bias[i]),
                           _per_head_lanes(a_log[i]), _per_head_lanes(d_skip[i]), _row(ssd_norm_g[i]), batch, seq)

        pool_bd = jax.scipy.linalg.block_diag(*[pool_w[i, g] for g in range(len(POOL_WINDOWS))]).astype(BF16)
        consts = (pool_bd, _row(pool_scale[i]), w_out[i].astype(BF16), _row(ln1_g[i]), _row(ln1_b[i]),
                  w_ff1[i].astype(BF16), w_ff2[i].astype(BF16), w_ple[i].astype(BF16), w_ple_gate[i].astype(BF16),
                  _row(ln2_g[i]), _row(ln2_b[i]))
        h = _mix_ffn(u_pool, o_diff, o_ssd, h, p2[i], consts, seq, alpha)
    return h.reshape(batch, seq, d_model)
```

```python
import functools
import math

import jax
import jax.numpy as jnp
from jax import lax
from jax.experimental import pallas as pl
from jax.experimental.pallas import tpu as pltpu

F32 = jnp.float32
BF16 = jnp.bfloat16

POOL_WINDOWS = (2, 4, 8, 16)
POOL_GROUP = 64
POOL_W = POOL_GROUP * len(POOL_WINDOWS)
DIFF_HEADS = 4
DIFF_QK_DIM = 32
DIFF_V_DIM = 2 * DIFF_QK_DIM
DIFF_W = DIFF_HEADS * DIFF_V_DIM
SSD_INNER = 512
SSD_HEAD_DIM = 64
SSD_HEADS = SSD_INNER // SSD_HEAD_DIM
SSD_GROUPS = 2
SSD_HEADS_PER_GROUP = SSD_HEADS // SSD_GROUPS
SSD_STATE = 128
CONV_K = 4
CONV_CH = SSD_INNER + 2 * SSD_GROUPS * SSD_STATE
CHUNK = 128
LN_EPS = 1e-5
LOG2E = 1.4426950408889634

LANES = 128
SUBLANES = 8
VMEM_LIMIT = 56 * 1024 * 1024

ROW_TILE = 512
ATT_TQ = 256
ATT_TK = 256
ATT_UNROLL = 8
ATT_SUM_ROWS = 16
POOL_HALO = 16

_OFF_POOL = 0
_OFF_Q = _OFF_POOL + POOL_W
_OFF_K = _OFF_Q + DIFF_W
_OFF_V = _OFF_K + DIFF_W
_OFF_Z = _OFF_V + DIFF_W
_OFF_XBC = _OFF_Z + SSD_INNER
_OFF_DT = _OFF_XBC + CONV_CH
_AUG_W = _OFF_DT + SSD_INNER


def _layer_norm(x, g, b):
    mu = jnp.mean(x, axis=-1, keepdims=True)
    xc = x - mu
    var = jnp.mean(xc * xc, axis=-1, keepdims=True)
    return xc * lax.rsqrt(var + LN_EPS) * g + b


def _const_spec(shape):
    nd = len(shape)
    return pl.BlockSpec(shape, lambda *_: (0,) * nd, pipeline_mode=pl.Buffered(1))


def _params(n_axes):
    return pltpu.CompilerParams(dimension_semantics=("arbitrary",) * n_axes, vmem_limit_bytes=VMEM_LIMIT)


def _inproj_kernel(apply_ln, qscale, x_ref, g_ref, b_ref, w_ref, *out_refs):
    if apply_ln:
        h_ref, up_ref, q_ref, k_ref, vt_ref, z_ref, xbc_ref, dt_ref = out_refs
        x = _layer_norm(x_ref[...], g_ref[...], b_ref[...])
        h_ref[...] = x
    else:
        up_ref, q_ref, k_ref, vt_ref, z_ref, xbc_ref, dt_ref = out_refs
        x = x_ref[...]
    xb = x.astype(BF16)

    def proj(lo, width):
        return jnp.dot(xb, w_ref[:, lo:lo + width], preferred_element_type=F32)

    up_ref[...] = proj(_OFF_POOL, POOL_W)
    q_ref[...] = (proj(_OFF_Q, DIFF_W) * qscale).astype(BF16)
    k_ref[...] = proj(_OFF_K, DIFF_W).astype(BF16)
    v = proj(_OFF_V, DIFF_W)
    tk = vt_ref.shape[-1]
    for s in range(vt_ref.shape[0]):
        vt_ref[s] = v[s * tk:(s + 1) * tk, :].T.astype(BF16)
    z_ref[...] = proj(_OFF_Z, SSD_INNER)
    xbc_ref[...] = proj(_OFF_XBC, CONV_CH)
    dt_ref[...] = proj(_OFF_DT, SSD_INNER)


def _in_projection(x, ln_g, ln_b, w_aug, apply_ln):
    t, d = x.shape
    tm = ROW_TILE
    nsub = tm // ATT_TK
    qscale = DIFF_QK_DIM ** -0.5 * LOG2E
    row = lambda w: pl.BlockSpec((tm, w), lambda i: (i, 0))
    out_shape = [
        jax.ShapeDtypeStruct((t, POOL_W), F32),
        jax.ShapeDtypeStruct((t, DIFF_W), BF16),
        jax.ShapeDtypeStruct((t, DIFF_W), BF16),
        jax.ShapeDtypeStruct((t // ATT_TK, DIFF_W, ATT_TK), BF16),
        jax.ShapeDtypeStruct((t, SSD_INNER), F32),
        jax.ShapeDtypeStruct((t, CONV_CH), F32),
        jax.ShapeDtypeStruct((t, SSD_INNER), F32),
    ]
    out_specs = [row(POOL_W), row(DIFF_W), row(DIFF_W),
                 pl.BlockSpec((nsub, DIFF_W, ATT_TK), lambda i: (i, 0, 0)),
                 row(SSD_INNER), row(CONV_CH), row(SSD_INNER)]
    if apply_ln:
        out_shape = [jax.ShapeDtypeStruct((t, d), F32)] + out_shape
        out_specs = [row(d)] + out_specs
    return pl.pallas_call(
        functools.partial(_inproj_kernel, apply_ln, qscale),
        grid=(t // tm,),
        in_specs=[row(d), _const_spec((1, d)), _const_spec((1, d)), _const_spec(w_aug.shape)],
        out_specs=out_specs,
        out_shape=out_shape,
        compiler_params=_params(1),
        name="in_proj_ln" if apply_ln else "in_proj",
    )(x, ln_g, ln_b, w_aug)


def _attn_kernel(lambda_init, q_ref, k_ref, vt_ref, lamv_ref, ng_ref, o_ref, acc_ref, s0_ref, s1_ref):
    qi = pl.program_id(2)
    tq = q_ref.shape[0]
    tk = vt_ref.shape[-1]
    n_comp = LANES // DIFF_QK_DIM
    qb = q_ref[...]
    lane = lax.broadcasted_iota(jnp.int32, (1, LANES), 1)
    zero = jnp.zeros_like(qb)
    q4 = jnp.concatenate(
        [jnp.where((lane >= DIFF_QK_DIM * c) & (lane < DIFF_QK_DIM * (c + 1)), qb, zero) for c in range(n_comp)],
        axis=0)
    wide = n_comp * tq
    acc_ref[...] = jnp.zeros_like(acc_ref)
    ones_rows = jnp.ones((ATT_SUM_ROWS, tk), BF16)

    def scores(j, s_ref, masked):
        kb = k_ref[pl.ds(pl.multiple_of(j * tk, tk), tk), :]
        s = lax.dot_general(kb, q4, (((1,), (1,)), ((), ())), preferred_element_type=F32)
        if masked:
            key_i = lax.broadcasted_iota(jnp.int32, (tk, wide), 0)
            query_i = lax.broadcasted_iota(jnp.int32, (tk, wide), 1) & (tq - 1)
            s = jnp.where(key_i <= query_i, s, -jnp.inf)
        s_ref[...] = s
        return jnp.max(s, axis=0, keepdims=True)

    def accumulate(j, s_ref, ms, maxima):
        vb = jnp.concatenate([vt_ref[j], ones_rows], axis=0)
        m_new = jnp.maximum(ms, maxima)
        alpha = jnp.exp2(ms - m_new)
        p = jnp.exp2(s_ref[...] - m_new).astype(BF16)
        acc_ref[...] = acc_ref[...] * alpha + jnp.dot(vb, p, preferred_element_type=F32)
        return m_new

    ms = jnp.full((1, wide), -jnp.inf, F32)
    maxima = scores(qi, s0_ref, True)

    def step(t, prev, s_write, s_read, ms, maxima):
        nxt = scores(t, s_write, False)
        return accumulate(prev, s_read, ms, maxima), nxt

    def pair(u, carry):
        ms, maxima, prev = carry
        ms, maxima = step(2 * u, prev, s1_ref, s0_ref, ms, maxima)
        ms, maxima = step(2 * u + 1, 2 * u, s0_ref, s1_ref, ms, maxima)
        return ms, maxima, 2 * u + 1

    def pairs(n):
        def run(u0, carry):
            for i in range(n):
                carry = pair(u0 + i, carry)
            return carry
        return run

    n_pairs = qi // 2
    group = ATT_UNROLL // 2
    carry = lax.fori_loop(0, n_pairs // group, lambda g, c: pairs(group)(g * group, c), (ms, maxima, qi))
    done = (n_pairs // group) * group
    while group > 1:
        group //= 2
        carry = lax.cond((n_pairs & group) != 0, functools.partial(pairs(group), done), lambda c: c, carry)
        done = done + (n_pairs & group)
    ms, maxima, prev = carry

    def odd_tail():
        ms1, maxima1 = step(qi - 1, prev, s1_ref, s0_ref, ms, maxima)
        accumulate(qi - 1, s1_ref, ms1, maxima1)

    lax.cond(qi % 2 == 1, odd_tail, lambda: (accumulate(prev, s0_ref, ms, maxima), None)[1])

    lamv = lamv_ref[...]
    lam = (jnp.exp(jnp.sum(lamv[0:1] * lamv[1:2], axis=-1, keepdims=True))
           - jnp.exp(jnp.sum(lamv[2:3] * lamv[3:4], axis=-1, keepdims=True)) + lambda_init)
    row = lax.broadcasted_iota(jnp.int32, (LANES, tq), 0)
    heads = []
    for h in range(2):
        norm = [acc_ref[0:LANES, c * tq:(c + 1) * tq] * (1.0 / acc_ref[LANES:LANES + 1, c * tq:(c + 1) * tq])
                for c in (2 * h, 2 * h + 1)]
        heads.append(norm[0] - lam * norm[1])
    first = row < DIFF_V_DIM
    ot = jnp.where(first, heads[0], heads[1])
    sq = ot * ot
    ms0 = jnp.sum(jnp.where(first, sq, 0.0), axis=0, keepdims=True) * (1.0 / DIFF_V_DIM)
    ms1 = jnp.sum(jnp.where(first, 0.0, sq), axis=0, keepdims=True) * (1.0 / DIFF_V_DIM)
    inv = jnp.where(first, lax.rsqrt(ms0 + LN_EPS), lax.rsqrt(ms1 + LN_EPS))
    y = (ot * inv).T * ng_ref[...] * (1.0 - lambda_init)
    o_ref[...] = y


def _diff_attention(q, k, vt, lamv, ng2, batch, seq, lambda_init):
    t = q.shape[0]
    tq, tk = ATT_TQ, ATT_TK
    nq = seq // tq
    nkb = seq // tk
    n_pairs = DIFF_W // LANES
    n_comp = LANES // DIFF_QK_DIM
    assert tq == tk and tq & (tq - 1) == 0, "square power-of-two blocks"
    return pl.pallas_call(
        functools.partial(_attn_kernel, lambda_init),
        grid=(batch, n_pairs, nq),
        in_specs=[
            pl.BlockSpec((tq, LANES), lambda b, hp, i: (b * nq + i, hp)),
            pl.BlockSpec((seq, LANES), lambda b, hp, i: (b, hp)),
            pl.BlockSpec((nkb, LANES, tk), lambda b, hp, i: (b, hp, 0)),
            _const_spec(lamv.shape),
            _const_spec(ng2.shape),
        ],
        out_specs=pl.BlockSpec((tq, LANES), lambda b, hp, i: (b * nq + i, hp)),
        out_shape=jax.ShapeDtypeStruct((t, DIFF_W), F32),
        scratch_shapes=[pltpu.VMEM((LANES + ATT_SUM_ROWS, n_comp * tq), F32),
                        pltpu.VMEM((tk, n_comp * tq), F32), pltpu.VMEM((tk, n_comp * tq), F32)],
        compiler_params=_params(3),
        name="diff_attention",
    )(q, k, vt, lamv, ng2)


def _split3(x):
    hi = x.astype(BF16)
    r = x - hi.astype(F32)
    mid = r.astype(BF16)
    lo = (r - mid.astype(F32)).astype(BF16)
    return hi, mid, lo


def _ssd_kernel(xbc_ref, z_ref, dt_ref, cw_ref, cb_ref, dtb_ref, alog_ref, dsk_ref, ng_ref, o_ref,
                ext_ref, state_ref):
    c = pl.program_id(1)
    lc = xbc_ref.shape[0]
    g_w = SSD_HEADS_PER_GROUP * SSD_HEAD_DIM
    n = SSD_STATE

    @pl.when(c == 0)
    def _():
        state_ref[...] = jnp.zeros_like(state_ref)
        ext_ref[0:SUBLANES, :] = jnp.zeros((SUBLANES, CONV_CH), F32)

    xin = xbc_ref[...]
    ext_ref[SUBLANES:, :] = xin
    cw = cw_ref[...]
    conv = cb_ref[...] + cw[CONV_K - 1:CONV_K] * xin
    for j in range(CONV_K - 1):
        off = SUBLANES - (CONV_K - 1) + j
        conv = conv + cw[j:j + 1] * ext_ref[off:off + lc, :]
    ext_ref[0:SUBLANES, :] = xin[lc - SUBLANES:, :]
    act = conv * jax.nn.sigmoid(conv)
    xs = act[:, :SSD_INNER]
    bm = act[:, SSD_INNER:SSD_INNER + SSD_GROUPS * n]
    cm = act[:, SSD_INNER + SSD_GROUPS * n:]

    dt_in = dt_ref[...] + dtb_ref[...]
    dt = jnp.maximum(dt_in, 0.0) + jnp.log(1.0 + jnp.exp(-jnp.abs(dt_in)))
    a_neg = -jnp.exp(alog_ref[...])
    dta = dt * a_neg
    ri = lax.broadcasted_iota(jnp.int32, (lc, lc), 0)
    ci = lax.broadcasted_iota(jnp.int32, (lc, lc), 1)
    causal = ri >= ci
    tri = jnp.where(causal, 1.0, 0.0).astype(BF16)
    acs = None
    for part in _split3(dta):
        term = jnp.dot(tri, part, preferred_element_type=F32)
        acs = term if acs is None else acs + term
    acs_last = acs[lc - 1:lc, :]
    xdt = xs * dt
    decayed = (xdt * jnp.exp(acs_last - acs)).astype(BF16)
    xdt_b = xdt.astype(BF16)
    exp_acs = jnp.exp(acs)
    chunk_decay = jnp.exp(acs_last)

    lane_g = lax.broadcasted_iota(jnp.int32, (1, g_w), 1)
    lane128 = lax.broadcasted_iota(jnp.int32, (1, LANES), 1)
    y_parts = []
    for g in range(SSD_GROUPS):
        bm_g = bm[:, g * n:(g + 1) * n]
        cm_g = cm[:, g * n:(g + 1) * n].astype(BF16)
        cb = lax.dot_general(cm_g, bm_g.astype(BF16), (((1,), (1,)), ((), ())), preferred_element_type=F32)
        xdt_g = xdt_b[:, g * g_w:(g + 1) * g_w]
        y_g = None
        for pair in range(SSD_HEADS_PER_GROUP // 2):
            lo = g * g_w + pair * LANES
            a_pair = acs[:, lo:lo + LANES]
            a_swap = pltpu.roll(a_pair, DIFF_V_DIM, 1)
            a_t = a_pair.T
            for sub in range(2):
                r = 2 * pair + sub
                own = (lane128 < SSD_HEAD_DIM) if sub == 0 else (lane128 >= SSD_HEAD_DIM)
                col = jnp.where(own, a_pair, a_swap)
                rowv = a_t[sub * SSD_HEAD_DIM:sub * SSD_HEAD_DIM + 1, :]
                lmat = jnp.where(causal, jnp.exp(col - rowv), 0.0)
                w = (cb * lmat).astype(BF16)
                sel = (lane_g >= r * SSD_HEAD_DIM) & (lane_g < (r + 1) * SSD_HEAD_DIM)
                contrib = jnp.dot(w, jnp.where(sel, xdt_g, jnp.zeros_like(xdt_g)), preferred_element_type=F32)
                y_g = contrib if y_g is None else y_g + contrib
        prev = state_ref[g]
        y_off = jnp.dot(cm_g, prev.astype(BF16), preferred_element_type=F32) * exp_acs[:, g * g_w:(g + 1) * g_w]
        st = jnp.dot(bm_g.T.astype(BF16), decayed[:, g * g_w:(g + 1) * g_w], preferred_element_type=F32)
        state_ref[g] = prev * chunk_decay[:, g * g_w:(g + 1) * g_w] + st
        y_parts.append(y_g + y_off)
    y = jnp.concatenate(y_parts, axis=1) + xs * dsk_ref[...]
    zz = z_ref[...]
    y = y * (zz * jax.nn.sigmoid(zz))
    y = y * lax.rsqrt(jnp.mean(y * y, axis=-1, keepdims=True) + LN_EPS)
    o_ref[...] = y * ng_ref[...]


def _ssd_mixer(xbc, z, dtf, conv_w, conv_b, dtb_f, alog_f, dsk_f, norm_g, batch, seq):
    t = xbc.shape[0]
    nc = seq // CHUNK
    row = lambda w: pl.BlockSpec((CHUNK, w), lambda b, c: (b * nc + c, 0))
    return pl.pallas_call(
        _ssd_kernel,
        grid=(batch, nc),
        in_specs=[row(CONV_CH), row(SSD_INNER), row(SSD_INNER),
                  _const_spec(conv_w.shape), _const_spec(conv_b.shape), _const_spec(dtb_f.shape),
                  _const_spec(alog_f.shape), _const_spec(dsk_f.shape), _const_spec(norm_g.shape)],
        out_specs=row(SSD_INNER),
        out_shape=jax.ShapeDtypeStruct((t, SSD_INNER), F32),
        scratch_shapes=[pltpu.VMEM((SUBLANES + CHUNK, CONV_CH), F32),
                        pltpu.VMEM((SSD_GROUPS, SSD_STATE, SSD_HEADS_PER_GROUP * SSD_HEAD_DIM), F32)],
        compiler_params=_params(2),
        name="ssd_mixer",
    )(xbc, z, dtf, conv_w, conv_b, dtb_f, alog_f, dsk_f, norm_g)


def _pooled(seq_blk, u_ref, halo_ref, lvl_ref):
    tm = u_ref.shape[0]
    base = 2 * POOL_HALO
    u = u_ref[...]
    halo = halo_ref[...]
    lvl_ref[:, 0:POOL_HALO, :] = jnp.zeros((len(POOL_WINDOWS), POOL_HALO, POOL_W), F32)
    lvl_ref[0, POOL_HALO:base, :] = jnp.where(seq_blk == 0, jnp.zeros_like(halo), halo)
    lvl_ref[0, base:, :] = u
    n_rows = tm + POOL_HALO
    sums = []
    for lv, shift in enumerate((1, 2, 4, 8)):
        cur = lvl_ref[lv, POOL_HALO:, :] + lvl_ref[lv, POOL_HALO - shift:POOL_HALO - shift + n_rows, :]
        if lv + 1 < len(POOL_WINDOWS):
            lvl_ref[lv + 1, POOL_HALO:, :] = cur
        sums.append(cur[POOL_HALO:, :])
    pos = (seq_blk * tm + lax.broadcasted_iota(jnp.int32, (tm, 1), 0) + 1).astype(F32)
    lane = lax.broadcasted_iota(jnp.int32, (1, POOL_W), 1)
    pooled = jnp.zeros((tm, POOL_W), F32)
    for gi, w in enumerate(POOL_WINDOWS):
        mean = sums[gi] / jnp.minimum(pos, float(w))
        pooled = jnp.where((lane >= gi * POOL_GROUP) & (lane < (gi + 1) * POOL_GROUP), mean, pooled)
    return pooled - u


def _mix_ffn_kernel(alpha, blocks_per_seq, u_ref, halo_ref, od_ref, os_ref, h_ref, p_ref, pw_ref, ps_ref, wo_ref,
                    g1_ref, b1_ref, w1_ref, w2_ref, wp_ref, wg_ref, g2_ref, b2_ref, o_ref, lvl_ref):
    seq_blk = pl.program_id(0) % blocks_per_seq
    pooled = _pooled(seq_blk, u_ref, halo_ref, lvl_ref)
    o_pool = jnp.dot(pooled.astype(BF16), pw_ref[...], preferred_element_type=F32) * ps_ref[...]
    cat = jnp.concatenate([o_pool.astype(BF16), od_ref[...].astype(BF16), os_ref[...].astype(BF16)], axis=1)
    mix = jnp.dot(cat, wo_ref[...], preferred_element_type=F32)
    h = _layer_norm(alpha * h_ref[...] + mix, g1_ref[...], b1_ref[...])

    hb = h.astype(BF16)
    hid = jnp.dot(hb, w1_ref[...], preferred_element_type=F32)
    hid = jnp.square(jnp.maximum(hid, 0.0)).astype(BF16)
    ff = jnp.dot(hid, w2_ref[...], preferred_element_type=F32)
    gate = jax.nn.sigmoid(jnp.dot(hb, wg_ref[...], preferred_element_type=F32))
    ple = jnp.dot(p_ref[...].astype(BF16), wp_ref[...], preferred_element_type=F32) * gate
    o_ref[...] = _layer_norm(alpha * h + ff + ple, g2_ref[...], b2_ref[...])


def _mix_ffn(u_pool, o_diff, o_ssd, h, p, layer, consts, seq, alpha):
    t, d = h.shape
    tm = ROW_TILE
    blocks_per_seq = seq // tm
    halo_per_tile = tm // POOL_HALO
    row = lambda w: pl.BlockSpec((tm, w), lambda i: (i, 0))
    return pl.pallas_call(
        functools.partial(_mix_ffn_kernel, alpha, blocks_per_seq),
        grid=(t // tm,),
        in_specs=[row(POOL_W),
                  pl.BlockSpec((POOL_HALO, POOL_W), lambda i: (jnp.maximum(i * halo_per_tile - 1, 0), 0)),
                  row(DIFF_W), row(SSD_INNER), row(d),
                  pl.BlockSpec((None, tm, p.shape[-1]), lambda i: (layer, i, 0))]
                 + [_const_spec(c.shape) for c in consts],
        out_specs=row(d),
        out_shape=jax.ShapeDtypeStruct((t, d), F32),
        scratch_shapes=[pltpu.VMEM((len(POOL_WINDOWS), tm + 2 * POOL_HALO, POOL_W), F32)],
        compiler_params=_params(1),
        name="mix_ffn",
    )(u_pool, u_pool, o_diff, o_ssd, h, p, *consts)


def _row(v):
    return v.reshape(1, -1).astype(F32)


def _per_head_lanes(v):
    return jnp.repeat(v.astype(F32), SSD_HEAD_DIM).reshape(1, SSD_INNER)


def kernel(x, p, ln_in_g, ln_in_b, w_in, pool_w, pool_scale, lam_q1, lam_k1, lam_q2, lam_k2, diff_norm_g, conv_w, conv_b, dt_bias, a_log, d_skip, ssd_norm_g, w_out, ln1_g, ln1_b, w_ff1, w_ff2, w_ple, w_ple_gate, ln2_g, ln2_b):
    batch, seq, d_model = x.shape
    depth = w_in.shape[0]
    t = batch * seq
    alpha = (2 * depth) ** 0.25
    h = x.reshape(t, d_model)
    p2 = p.reshape(depth, t, p.shape[-1])
    for i in range(depth):
        lambda_init = 0.8 - 0.6 * math.exp(-0.3 * i)
        w = w_in[i]
        w_aug = jnp.concatenate([w[:, :_OFF_DT], jnp.repeat(w[:, _OFF_DT:], SSD_HEAD_DIM, axis=1)], axis=1).astype(BF16)
        outs = _in_projection(h, _row(ln_in_g), _row(ln_in_b), w_aug, apply_ln=(i == 0))
        if i == 0:
            h, outs = outs[0], outs[1:]
        u_pool, q, k, vt, z, xbc, dtf = outs

        lamv = jnp.zeros((4, LANES), F32).at[:, :DIFF_QK_DIM].set(
            jnp.stack([lam_q1[i], lam_k1[i], lam_q2[i], lam_k2[i]]).astype(F32))
        ng2 = jnp.tile(diff_norm_g[i].astype(F32), LANES // DIFF_V_DIM).reshape(1, LANES)
        o_diff = _diff_attention(q, k, vt, lamv, ng2, batch, seq, lambda_init)

        o_ssd = _ssd_mixer(xbc, z, dtf, conv_w[i].astype(F32), _row(conv_b[i]), _per_head_lanes(dt_bias[i]),
                           _per_head_lanes(a_log[i]), _per_head_lanes(d_skip[i]), _row(ssd_norm_g[i]), batch, seq)

        pool_bd = jax.scipy.linalg.block_diag(*[pool_w[i, g] for g in range(len(POOL_WINDOWS))]).astype(BF16)
        consts = (pool_bd, _row(pool_scale[i]), w_out[i].astype(BF16), _row(ln1_g[i]), _row(ln1_b[i]),
                  w_ff1[i].astype(BF16), w_ff2[i].astype(BF16), w_ple[i].astype(BF16), w_ple_gate[i].astype(BF16),
                  _row(ln2_g[i]), _row(ln2_b[i]))
        h = _mix_ffn(u_pool, o_diff, o_ssd, h, p2, i, consts, seq, alpha)
    return h.reshape(batch, seq, d_model)
```

```python
import functools
import math

import jax
import jax.numpy as jnp
from jax import lax
from jax.experimental import pallas as pl
from jax.experimental.pallas import tpu as pltpu

F32 = jnp.float32
BF16 = jnp.bfloat16

POOL_WINDOWS = (2, 4, 8, 16)
POOL_GROUP = 64
POOL_W = POOL_GROUP * len(POOL_WINDOWS)
DIFF_HEADS = 4
DIFF_QK_DIM = 32
DIFF_V_DIM = 2 * DIFF_QK_DIM
DIFF_W = DIFF_HEADS * DIFF_V_DIM
SSD_INNER = 512
SSD_HEAD_DIM = 64
SSD_HEADS = SSD_INNER // SSD_HEAD_DIM
SSD_GROUPS = 2
SSD_HEADS_PER_GROUP = SSD_HEADS // SSD_GROUPS
SSD_STATE = 128
CONV_K = 4
CONV_CH = SSD_INNER + 2 * SSD_GROUPS * SSD_STATE
CHUNK = 128
LN_EPS = 1e-5
LOG2E = 1.4426950408889634

LANES = 128
SUBLANES = 8
VMEM_LIMIT = 56 * 1024 * 1024

ROW_TILE = 512
ATT_TQ = 256
ATT_TK = 256
ATT_UNROLL = 8
ATT_SUM_ROWS = 16
POOL_HALO = 16

_OFF_POOL = 0
_OFF_Q = _OFF_POOL + POOL_W
_OFF_K = _OFF_Q + DIFF_W
_OFF_V = _OFF_K + DIFF_W
_OFF_Z = _OFF_V + DIFF_W
_OFF_XBC = _OFF_Z + SSD_INNER
_OFF_DT = _OFF_XBC + CONV_CH
_AUG_W = _OFF_DT + SSD_INNER


def _layer_norm(x, g, b):
    mu = jnp.mean(x, axis=-1, keepdims=True)
    xc = x - mu
    var = jnp.mean(xc * xc, axis=-1, keepdims=True)
    return xc * lax.rsqrt(var + LN_EPS) * g + b


def _const_spec(shape):
    nd = len(shape)
    return pl.BlockSpec(shape, lambda *_: (0,) * nd, pipeline_mode=pl.Buffered(1))


def _params(n_axes):
    return pltpu.CompilerParams(dimension_semantics=("arbitrary",) * n_axes, vmem_limit_bytes=VMEM_LIMIT)


def _inproj_kernel(apply_ln, qscale, x_ref, g_ref, b_ref, w_ref, *out_refs):
    if apply_ln:
        h_ref, up_ref, q_ref, k_ref, vt_ref, z_ref, xbc_ref, dt_ref = out_refs
        x = _layer_norm(x_ref[...], g_ref[...], b_ref[...])
        h_ref[...] = x
    else:
        up_ref, q_ref, k_ref, vt_ref, z_ref, xbc_ref, dt_ref = out_refs
        x = x_ref[...]
    xb = x.astype(BF16)

    def proj(lo, width):
        return jnp.dot(xb, w_ref[:, lo:lo + width], preferred_element_type=F32)

    up_ref[...] = proj(_OFF_POOL, POOL_W)
    q_ref[...] = (proj(_OFF_Q, DIFF_W) * qscale).astype(BF16)
    k_ref[...] = proj(_OFF_K, DIFF_W).astype(BF16)
    v = proj(_OFF_V, DIFF_W)
    tk = vt_ref.shape[-1]
    for s in range(vt_ref.shape[0]):
        vt_ref[s] = v[s * tk:(s + 1) * tk, :].T.astype(BF16)
    z_ref[...] = proj(_OFF_Z, SSD_INNER)
    xbc_ref[...] = proj(_OFF_XBC, CONV_CH)
    dt_ref[...] = proj(_OFF_DT, SSD_INNER)


def _in_projection(x, ln_g, ln_b, w_aug, apply_ln):
    t, d = x.shape
    tm = ROW_TILE
    nsub = tm // ATT_TK
    qscale = DIFF_QK_DIM ** -0.5 * LOG2E
    row = lambda w: pl.BlockSpec((tm, w), lambda i: (i, 0))
    out_shape = [
        jax.ShapeDtypeStruct((t, POOL_W), F32),
        jax.ShapeDtypeStruct((t, DIFF_W), BF16),
        jax.ShapeDtypeStruct((t, DIFF_W), BF16),
        jax.ShapeDtypeStruct((t // ATT_TK, DIFF_W, ATT_TK), BF16),
        jax.ShapeDtypeStruct((t, SSD_INNER), F32),
        jax.ShapeDtypeStruct((t, CONV_CH), F32),
        jax.ShapeDtypeStruct((t, SSD_INNER), F32),
    ]
    out_specs = [row(POOL_W), row(DIFF_W), row(DIFF_W),
                 pl.BlockSpec((nsub, DIFF_W, ATT_TK), lambda i: (i, 0, 0)),
                 row(SSD_INNER), row(CONV_CH), row(SSD_INNER)]
    if apply_ln:
        out_shape = [jax.ShapeDtypeStruct((t, d), F32)] + out_shape
        out_specs = [row(d)] + out_specs
    return pl.pallas_call(
        functools.partial(_inproj_kernel, apply_ln, qscale),
        grid=(t // tm,),
        in_specs=[row(d), _const_spec((1, d)), _const_spec((1, d)), _const_spec(w_aug.shape)],
        out_specs=out_specs,
        out_shape=out_shape,
        compiler_params=_params(1),
        name="in_proj_ln" if apply_ln else "in_proj",
    )(x, ln_g, ln_b, w_aug)


def _attn_kernel(lambda_init, q_ref, k_ref, vt_ref, lamv_ref, ng_ref, o_ref, acc_ref, s0_ref, s1_ref):
    qi = pl.program_id(2)
    tq = q_ref.shape[0]
    tk = vt_ref.shape[-1]
    n_comp = LANES // DIFF_QK_DIM
    qb = q_ref[...]
    lane = lax.broadcasted_iota(jnp.int32, (1, LANES), 1)
    zero = jnp.zeros_like(qb)
    q4 = jnp.concatenate(
        [jnp.where((lane >= DIFF_QK_DIM * c) & (lane < DIFF_QK_DIM * (c + 1)), qb, zero) for c in range(n_comp)],
        axis=0)
    wide = n_comp * tq
    acc_ref[...] = jnp.zeros_like(acc_ref)
    ones_rows = jnp.ones((ATT_SUM_ROWS, tk), BF16)

    def scores(j, s_ref, masked):
        kb = k_ref[pl.ds(pl.multiple_of(j * tk, tk), tk), :]
        s = lax.dot_general(kb, q4, (((1,), (1,)), ((), ())), preferred_element_type=F32)
        if masked:
            key_i = lax.broadcasted_iota(jnp.int32, (tk, wide), 0)
            query_i = lax.broadcasted_iota(jnp.int32, (tk, wide), 1) & (tq - 1)
            s = jnp.where(key_i <= query_i, s, -jnp.inf)
        s_ref[...] = s
        return jnp.max(s, axis=0, keepdims=True)

    def accumulate(j, s_ref, ms, maxima):
        m_new = jnp.maximum(ms, maxima)
        alpha = jnp.exp2(ms - m_new)
        p = jnp.exp2(s_ref[...] - m_new).astype(BF16)
        half = wide // 2
        for h in range(2):
            vb = jnp.concatenate([vt_ref[j, h * DIFF_V_DIM:(h + 1) * DIFF_V_DIM, :], ones_rows], axis=0)
            cols = slice(h * half, (h + 1) * half)
            acc_ref[h] = acc_ref[h] * alpha[:, cols] + jnp.dot(vb, p[:, cols], preferred_element_type=F32)
        return m_new

    ms = jnp.full((1, wide), -jnp.inf, F32)
    maxima = scores(qi, s0_ref, True)

    def step(t, prev, s_write, s_read, ms, maxima):
        nxt = scores(t, s_write, False)
        return accumulate(prev, s_read, ms, maxima), nxt

    def pair(u, carry):
        ms, maxima, prev = carry
        ms, maxima = step(2 * u, prev, s1_ref, s0_ref, ms, maxima)
        ms, maxima = step(2 * u + 1, 2 * u, s0_ref, s1_ref, ms, maxima)
        return ms, maxima, 2 * u + 1

    def pairs(n):
        def run(u0, carry):
            for i in range(n):
                carry = pair(u0 + i, carry)
            return carry
        return run

    n_pairs = qi // 2
    group = ATT_UNROLL // 2
    carry = lax.fori_loop(0, n_pairs // group, lambda g, c: pairs(group)(g * group, c), (ms, maxima, qi))
    done = (n_pairs // group) * group
    while group > 1:
        group //= 2
        carry = lax.cond((n_pairs & group) != 0, functools.partial(pairs(group), done), lambda c: c, carry)
        done = done + (n_pairs & group)
    ms, maxima, prev = carry

    def odd_tail():
        ms1, maxima1 = step(qi - 1, prev, s1_ref, s0_ref, ms, maxima)
        accumulate(qi - 1, s1_ref, ms1, maxima1)

    lax.cond(qi % 2 == 1, odd_tail, lambda: (accumulate(prev, s0_ref, ms, maxima), None)[1])

    lamv = lamv_ref[...]
    lam = (jnp.exp(jnp.sum(lamv[0:1] * lamv[1:2], axis=-1, keepdims=True))
           - jnp.exp(jnp.sum(lamv[2:3] * lamv[3:4], axis=-1, keepdims=True)) + lambda_init)
    outs = []
    for h in range(2):
        norm = [acc_ref[h, 0:DIFF_V_DIM, c * tq:(c + 1) * tq]
                * (1.0 / acc_ref[h, DIFF_V_DIM:DIFF_V_DIM + 1, c * tq:(c + 1) * tq]) for c in range(2)]
        o_h = norm[0] - lam * norm[1]
        mean_sq = jnp.mean(o_h * o_h, axis=0, keepdims=True)
        outs.append(o_h * lax.rsqrt(mean_sq + LN_EPS))
    y = jnp.concatenate(outs, axis=0).T * ng_ref[...] * (1.0 - lambda_init)
    o_ref[...] = y


def _diff_attention(q, k, vt, lamv, ng2, batch, seq, lambda_init):
    t = q.shape[0]
    tq, tk = ATT_TQ, ATT_TK
    nq = seq // tq
    nkb = seq // tk
    n_pairs = DIFF_W // LANES
    n_comp = LANES // DIFF_QK_DIM
    assert tq == tk and tq & (tq - 1) == 0, "square power-of-two blocks"
    return pl.pallas_call(
        functools.partial(_attn_kernel, lambda_init),
        grid=(batch, n_pairs, nq),
        in_specs=[
            pl.BlockSpec((tq, LANES), lambda b, hp, i: (b * nq + i, hp)),
            pl.BlockSpec((seq, LANES), lambda b, hp, i: (b, hp)),
            pl.BlockSpec((nkb, LANES, tk), lambda b, hp, i: (b, hp, 0)),
            _const_spec(lamv.shape),
            _const_spec(ng2.shape),
        ],
        out_specs=pl.BlockSpec((tq, LANES), lambda b, hp, i: (b * nq + i, hp)),
        out_shape=jax.ShapeDtypeStruct((t, DIFF_W), F32),
        scratch_shapes=[pltpu.VMEM((2, DIFF_V_DIM + ATT_SUM_ROWS, n_comp * tq // 2), F32),
                        pltpu.VMEM((tk, n_comp * tq), F32), pltpu.VMEM((tk, n_comp * tq), F32)],
        compiler_params=_params(3),
        name="diff_attention",
    )(q, k, vt, lamv, ng2)


def _split3(x):
    hi = x.astype(BF16)
    r = x - hi.astype(F32)
    mid = r.astype(BF16)
    lo = (r - mid.astype(F32)).astype(BF16)
    return hi, mid, lo


def _ssd_kernel(xbc_ref, z_ref, dt_ref, cw_ref, cb_ref, dtb_ref, alog_ref, dsk_ref, ng_ref, o_ref,
                ext_ref, state_ref):
    c = pl.program_id(1)
    lc = xbc_ref.shape[0]
    g_w = SSD_HEADS_PER_GROUP * SSD_HEAD_DIM
    n = SSD_STATE

    @pl.when(c == 0)
    def _():
        state_ref[...] = jnp.zeros_like(state_ref)
        ext_ref[0:SUBLANES, :] = jnp.zeros((SUBLANES, CONV_CH), F32)

    xin = xbc_ref[...]
    ext_ref[SUBLANES:, :] = xin
    cw = cw_ref[...]
    conv = cb_ref[...] + cw[CONV_K - 1:CONV_K] * xin
    for j in range(CONV_K - 1):
        off = SUBLANES - (CONV_K - 1) + j
        conv = conv + cw[j:j + 1] * ext_ref[off:off + lc, :]
    ext_ref[0:SUBLANES, :] = xin[lc - SUBLANES:, :]
    act = conv * jax.nn.sigmoid(conv)
    xs = act[:, :SSD_INNER]
    bm = act[:, SSD_INNER:SSD_INNER + SSD_GROUPS * n]
    cm = act[:, SSD_INNER + SSD_GROUPS * n:]

    dt_in = dt_ref[...] + dtb_ref[...]
    dt = jnp.maximum(dt_in, 0.0) + jnp.log(1.0 + jnp.exp(-jnp.abs(dt_in)))
    a_neg = -jnp.exp(alog_ref[...])
    dta = dt * a_neg
    ri = lax.broadcasted_iota(jnp.int32, (lc, lc), 0)
    ci = lax.broadcasted_iota(jnp.int32, (lc, lc), 1)
    causal = ri >= ci
    tri = jnp.where(causal, 1.0, 0.0).astype(BF16)
    acs = None
    for part in _split3(dta):
        term = jnp.dot(tri, part, preferred_element_type=F32)
        acs = term if acs is None else acs + term
    acs_last = acs[lc - 1:lc, :]
    xdt = xs * dt
    decayed = (xdt * jnp.exp(acs_last - acs)).astype(BF16)
    xdt_b = xdt.astype(BF16)
    exp_acs = jnp.exp(acs)
    chunk_decay = jnp.exp(acs_last)

    lane_g = lax.broadcasted_iota(jnp.int32, (1, g_w), 1)
    lane128 = lax.broadcasted_iota(jnp.int32, (1, LANES), 1)
    y_parts = []
    for g in range(SSD_GROUPS):
        bm_g = bm[:, g * n:(g + 1) * n]
        cm_g = cm[:, g * n:(g + 1) * n].astype(BF16)
        cb = lax.dot_general(cm_g, bm_g.astype(BF16), (((1,), (1,)), ((), ())), preferred_element_type=F32)
        xdt_g = xdt_b[:, g * g_w:(g + 1) * g_w]
        y_g = None
        for pair in range(SSD_HEADS_PER_GROUP // 2):
            lo = g * g_w + pair * LANES
            a_pair = acs[:, lo:lo + LANES]
            a_swap = pltpu.roll(a_pair, DIFF_V_DIM, 1)
            a_t = a_pair.T
            for sub in range(2):
                r = 2 * pair + sub
                own = (lane128 < SSD_HEAD_DIM) if sub == 0 else (lane128 >= SSD_HEAD_DIM)
                col = jnp.where(own, a_pair, a_swap)
                rowv = a_t[sub * SSD_HEAD_DIM:sub * SSD_HEAD_DIM + 1, :]
                lmat = jnp.where(causal, jnp.exp(col - rowv), 0.0)
                w = (cb * lmat).astype(BF16)
                sel = (lane_g >= r * SSD_HEAD_DIM) & (lane_g < (r + 1) * SSD_HEAD_DIM)
                contrib = jnp.dot(w, jnp.where(sel, xdt_g, jnp.zeros_like(xdt_g)), preferred_element_type=F32)
                y_g = contrib if y_g is None else y_g + contrib
        prev = state_ref[g]
        y_off = jnp.dot(cm_g, prev.astype(BF16), preferred_element_type=F32) * exp_acs[:, g * g_w:(g + 1) * g_w]
        st = jnp.dot(bm_g.T.astype(BF16), decayed[:, g * g_w:(g + 1) * g_w], preferred_element_type=F32)
        state_ref[g] = prev * chunk_decay[:, g * g_w:(g + 1) * g_w] + st
        y_parts.append(y_g + y_off)
    y = jnp.concatenate(y_parts, axis=1) + xs * dsk_ref[...]
    zz = z_ref[...]
    y = y * (zz * jax.nn.sigmoid(zz))
    y = y * lax.rsqrt(jnp.mean(y * y, axis=-1, keepdims=True) + LN_EPS)
    o_ref[...] = y * ng_ref[...]


def _ssd_mixer(xbc, z, dtf, conv_w, conv_b, dtb_f, alog_f, dsk_f, norm_g, batch, seq):
    t = xbc.shape[0]
    nc = seq // CHUNK
    row = lambda w: pl.BlockSpec((CHUNK, w), lambda b, c: (b * nc + c, 0))
    return pl.pallas_call(
        _ssd_kernel,
        grid=(batch, nc),
        in_specs=[row(CONV_CH), row(SSD_INNER), row(SSD_INNER),
                  _const_spec(conv_w.shape), _const_spec(conv_b.shape), _const_spec(dtb_f.shape),
                  _const_spec(alog_f.shape), _const_spec(dsk_f.shape), _const_spec(norm_g.shape)],
        out_specs=row(SSD_INNER),
        out_shape=jax.ShapeDtypeStruct((t, SSD_INNER), F32),
        scratch_shapes=[pltpu.VMEM((SUBLANES + CHUNK, CONV_CH), F32),
                        pltpu.VMEM((SSD_GROUPS, SSD_STATE, SSD_HEADS_PER_GROUP * SSD_HEAD_DIM), F32)],
        compiler_params=_params(2),
        name="ssd_mixer",
    )(xbc, z, dtf, conv_w, conv_b, dtb_f, alog_f, dsk_f, norm_g)


def _pooled(seq_blk, u_ref, halo_ref, lvl_ref):
    tm = u_ref.shape[0]
    base = 2 * POOL_HALO
    u = u_ref[...]
    halo = halo_ref[...]
    lvl_ref[:, 0:POOL_HALO, :] = jnp.zeros((len(POOL_WINDOWS), POOL_HALO, POOL_W), F32)
    lvl_ref[0, POOL_HALO:base, :] = jnp.where(seq_blk == 0, jnp.zeros_like(halo), halo)
    lvl_ref[0, base:, :] = u
    n_rows = tm + POOL_HALO
    sums = []
    for lv, shift in enumerate((1, 2, 4, 8)):
        cur = lvl_ref[lv, POOL_HALO:, :] + lvl_ref[lv, POOL_HALO - shift:POOL_HALO - shift + n_rows, :]
        if lv + 1 < len(POOL_WINDOWS):
            lvl_ref[lv + 1, POOL_HALO:, :] = cur
        sums.append(cur[POOL_HALO:, :])
    pos = (seq_blk * tm + lax.broadcasted_iota(jnp.int32, (tm, 1), 0) + 1).astype(F32)
    lane = lax.broadcasted_iota(jnp.int32, (1, POOL_W), 1)
    pooled = jnp.zeros((tm, POOL_W), F32)
    for gi, w in enumerate(POOL_WINDOWS):
        mean = sums[gi] / jnp.minimum(pos, float(w))
        pooled = jnp.where((lane >= gi * POOL_GROUP) & (lane < (gi + 1) * POOL_GROUP), mean, pooled)
    return pooled - u


def _mix_ffn_kernel(alpha, blocks_per_seq, u_ref, halo_ref, od_ref, os_ref, h_ref, p_ref, pw_ref, ps_ref, wo_ref,
                    g1_ref, b1_ref, w1_ref, w2_ref, wp_ref, wg_ref, g2_ref, b2_ref, o_ref, lvl_ref):
    seq_blk = pl.program_id(0) % blocks_per_seq
    pooled = _pooled(seq_blk, u_ref, halo_ref, lvl_ref)
    o_pool = jnp.dot(pooled.astype(BF16), pw_ref[...], preferred_element_type=F32) * ps_ref[...]
    cat = jnp.concatenate([o_pool.astype(BF16), od_ref[...].astype(BF16), os_ref[...].astype(BF16)], axis=1)
    mix = jnp.dot(cat, wo_ref[...], preferred_element_type=F32)
    h = _layer_norm(alpha * h_ref[...] + mix, g1_ref[...], b1_ref[...])

    hb = h.astype(BF16)
    hid = jnp.dot(hb, w1_ref[...], preferred_element_type=F32)
    hid = jnp.square(jnp.maximum(hid, 0.0)).astype(BF16)
    ff = jnp.dot(hid, w2_ref[...], preferred_element_type=F32)
    gate = jax.nn.sigmoid(jnp.dot(hb, wg_ref[...], preferred_element_type=F32))
    ple = jnp.dot(p_ref[...].astype(BF16), wp_ref[...], preferred_element_type=F32) * gate
    o_ref[...] = _layer_norm(alpha * h + ff + ple, g2_ref[...], b2_ref[...])


def _mix_ffn(u_pool, o_diff, o_ssd, h, p, layer, consts, seq, alpha):
    t, d = h.shape
    tm = ROW_TILE
    blocks_per_seq = seq // tm
    halo_per_tile = tm // POOL_HALO
    row = lambda w: pl.BlockSpec((tm, w), lambda i: (i, 0))
    return pl.pallas_call(
        functools.partial(_mix_ffn_kernel, alpha, blocks_per_seq),
        grid=(t // tm,),
        in_specs=[row(POOL_W),
                  pl.BlockSpec((POOL_HALO, POOL_W), lambda i: (jnp.maximum(i * halo_per_tile - 1, 0), 0)),
                  row(DIFF_W), row(SSD_INNER), row(d),
                  pl.BlockSpec((None, tm, p.shape[-1]), lambda i: (layer, i, 0))]
                 + [_const_spec(c.shape) for c in consts],
        out_specs=row(d),
        out_shape=jax.ShapeDtypeStruct((t, d), F32),
        scratch_shapes=[pltpu.VMEM((len(POOL_WINDOWS), tm + 2 * POOL_HALO, POOL_W), F32)],
        compiler_params=_params(1),
        name="mix_ffn",
    )(u_pool, u_pool, o_diff, o_ssd, h, p, *consts)


def _row(v):
    return v.reshape(1, -1).astype(F32)


def _per_head_lanes(v):
    return jnp.repeat(v.astype(F32), SSD_HEAD_DIM).reshape(1, SSD_INNER)


def kernel(x, p, ln_in_g, ln_in_b, w_in, pool_w, pool_scale, lam_q1, lam_k1, lam_q2, lam_k2, diff_norm_g, conv_w, conv_b, dt_bias, a_log, d_skip, ssd_norm_g, w_out, ln1_g, ln1_b, w_ff1, w_ff2, w_ple, w_ple_gate, ln2_g, ln2_b):
    batch, seq, d_model = x.shape
    depth = w_in.shape[0]
    t = batch * seq
    alpha = (2 * depth) ** 0.25
    h = x.reshape(t, d_model)
    p2 = p.reshape(depth, t, p.shape[-1])
    for i in range(depth):
        lambda_init = 0.8 - 0.6 * math.exp(-0.3 * i)
        w = w_in[i]
        w_aug = jnp.concatenate([w[:, :_OFF_DT], jnp.repeat(w[:, _OFF_DT:], SSD_HEAD_DIM, axis=1)], axis=1).astype(BF16)
        outs = _in_projection(h, _row(ln_in_g), _row(ln_in_b), w_aug, apply_ln=(i == 0))
        if i == 0:
            h, outs = outs[0], outs[1:]
        u_pool, q, k, vt, z, xbc, dtf = outs

        lamv = jnp.zeros((4, LANES), F32).at[:, :DIFF_QK_DIM].set(
            jnp.stack([lam_q1[i], lam_k1[i], lam_q2[i], lam_k2[i]]).astype(F32))
        ng2 = jnp.tile(diff_norm_g[i].astype(F32), LANES // DIFF_V_DIM).reshape(1, LANES)
        o_diff = _diff_attention(q, k, vt, lamv, ng2, batch, seq, lambda_init)

        o_ssd = _ssd_mixer(xbc, z, dtf, conv_w[i].astype(F32), _row(conv_b[i]), _per_head_lanes(dt_bias[i]),
                           _per_head_lanes(a_log[i]), _per_head_lanes(d_skip[i]), _row(ssd_norm_g[i]), batch, seq)

        pool_bd = jax.scipy.linalg.block_diag(*[pool_w[i, g] for g in range(len(POOL_WINDOWS))]).astype(BF16)
        consts = (pool_bd, _row(pool_scale[i]), w_out[i].astype(BF16), _row(ln1_g[i]), _row(ln1_b[i]),
                  w_ff1[i].astype(BF16), w_ff2[i].astype(BF16), w_ple[i].astype(BF16), w_ple_gate[i].astype(BF16),
                  _row(ln2_g[i]), _row(ln2_b[i]))
        h = _mix_ffn(u_pool, o_diff, o_ssd, h, p2, i, consts, seq, alpha)
    return h.reshape(batch, seq, d_model)
```

```python
import functools
import math

import jax
import jax.numpy as jnp
from jax import lax
from jax.experimental import pallas as pl
from jax.experimental.pallas import tpu as pltpu

F32 = jnp.float32
BF16 = jnp.bfloat16

POOL_WINDOWS = (2, 4, 8, 16)
POOL_GROUP = 64
POOL_W = POOL_GROUP * len(POOL_WINDOWS)
DIFF_HEADS = 4
DIFF_QK_DIM = 32
DIFF_V_DIM = 2 * DIFF_QK_DIM
DIFF_W = DIFF_HEADS * DIFF_V_DIM
SSD_INNER = 512
SSD_HEAD_DIM = 64
SSD_HEADS = SSD_INNER // SSD_HEAD_DIM
SSD_GROUPS = 2
SSD_HEADS_PER_GROUP = SSD_HEADS // SSD_GROUPS
SSD_STATE = 128
CONV_K = 4
CONV_CH = SSD_INNER + 2 * SSD_GROUPS * SSD_STATE
CHUNK = 128
LN_EPS = 1e-5
LOG2E = 1.4426950408889634

LANES = 128
SUBLANES = 8
VMEM_LIMIT = 56 * 1024 * 1024

ROW_TILE = 512
ATT_TQ = 256
ATT_TK = 256
ATT_UNROLL = 8
ATT_SUM_ROWS = 16
POOL_HALO = 16

_OFF_POOL = 0
_OFF_Q = _OFF_POOL + POOL_W
_OFF_K = _OFF_Q + DIFF_W
_OFF_V = _OFF_K + DIFF_W
_OFF_Z = _OFF_V + DIFF_W
_OFF_XBC = _OFF_Z + SSD_INNER
_OFF_DT = _OFF_XBC + CONV_CH
_AUG_W = _OFF_DT + SSD_INNER


def _layer_norm(x, g, b):
    mu = jnp.mean(x, axis=-1, keepdims=True)
    xc = x - mu
    var = jnp.mean(xc * xc, axis=-1, keepdims=True)
    return xc * lax.rsqrt(var + LN_EPS) * g + b


def _const_spec(shape):
    nd = len(shape)
    return pl.BlockSpec(shape, lambda *_: (0,) * nd, pipeline_mode=pl.Buffered(1))


def _params(n_axes):
    return pltpu.CompilerParams(dimension_semantics=("arbitrary",) * n_axes, vmem_limit_bytes=VMEM_LIMIT)


def _inproj_kernel(apply_ln, qscale, x_ref, g_ref, b_ref, w_ref, *out_refs):
    if apply_ln:
        h_ref, up_ref, q_ref, k_ref, vt_ref, z_ref, xbc_ref, dt_ref = out_refs
        x = _layer_norm(x_ref[...], g_ref[...], b_ref[...])
        h_ref[...] = x
    else:
        up_ref, q_ref, k_ref, vt_ref, z_ref, xbc_ref, dt_ref = out_refs
        x = x_ref[...]
    xb = x.astype(BF16)

    def proj(lo, width):
        return jnp.dot(xb, w_ref[:, lo:lo + width], preferred_element_type=F32)

    up_ref[...] = proj(_OFF_POOL, POOL_W)
    q_ref[...] = (proj(_OFF_Q, DIFF_W) * qscale).astype(BF16)
    k_ref[...] = proj(_OFF_K, DIFF_W).astype(BF16)
    v = proj(_OFF_V, DIFF_W)
    tk = vt_ref.shape[-1]
    for s in range(vt_ref.shape[0]):
        vt_ref[s] = v[s * tk:(s + 1) * tk, :].T.astype(BF16)
    z_ref[...] = proj(_OFF_Z, SSD_INNER)
    xbc_ref[...] = proj(_OFF_XBC, CONV_CH)
    dt_ref[...] = proj(_OFF_DT, SSD_INNER)


def _in_projection(x, ln_g, ln_b, w_aug, apply_ln):
    t, d = x.shape
    tm = ROW_TILE
    nsub = tm // ATT_TK
    qscale = DIFF_QK_DIM ** -0.5 * LOG2E
    row = lambda w: pl.BlockSpec((tm, w), lambda i: (i, 0))
    out_shape = [
        jax.ShapeDtypeStruct((t, POOL_W), F32),
        jax.ShapeDtypeStruct((t, DIFF_W), BF16),
        jax.ShapeDtypeStruct((t, DIFF_W), BF16),
        jax.ShapeDtypeStruct((t // ATT_TK, DIFF_W, ATT_TK), BF16),
        jax.ShapeDtypeStruct((t, SSD_INNER), F32),
        jax.ShapeDtypeStruct((t, CONV_CH), F32),
        jax.ShapeDtypeStruct((t, SSD_INNER), F32),
    ]
    out_specs = [row(POOL_W), row(DIFF_W), row(DIFF_W),
                 pl.BlockSpec((nsub, DIFF_W, ATT_TK), lambda i: (i, 0, 0)),
                 row(SSD_INNER), row(CONV_CH), row(SSD_INNER)]
    if apply_ln:
        out_shape = [jax.ShapeDtypeStruct((t, d), F32)] + out_shape
        out_specs = [row(d)] + out_specs
    return pl.pallas_call(
        functools.partial(_inproj_kernel, apply_ln, qscale),
        grid=(t // tm,),
        in_specs=[row(d), _const_spec((1, d)), _const_spec((1, d)), _const_spec(w_aug.shape)],
        out_specs=out_specs,
        out_shape=out_shape,
        compiler_params=_params(1),
        name="in_proj_ln" if apply_ln else "in_proj",
    )(x, ln_g, ln_b, w_aug)


def _attn_kernel(lambda_init, q_ref, k_ref, vt_ref, lamv_ref, ng_ref, o_ref, acc_ref, s0_ref, s1_ref):
    qi = pl.program_id(2)
    tq = q_ref.shape[0]
    tk = vt_ref.shape[-1]
    n_comp = LANES // DIFF_QK_DIM
    qb = q_ref[...]
    lane = lax.broadcasted_iota(jnp.int32, (1, LANES), 1)
    zero = jnp.zeros_like(qb)
    q4 = jnp.concatenate(
        [jnp.where((lane >= DIFF_QK_DIM * c) & (lane < DIFF_QK_DIM * (c + 1)), qb, zero) for c in range(n_comp)],
        axis=0)
    wide = n_comp * tq
    acc_ref[...] = jnp.zeros_like(acc_ref)
    ones_rows = jnp.ones((ATT_SUM_ROWS, tk), BF16)

    def scores(j, s_ref, masked):
        kb = k_ref[pl.ds(pl.multiple_of(j * tk, tk), tk), :]
        s = lax.dot_general(kb, q4, (((1,), (1,)), ((), ())), preferred_element_type=F32)
        if masked:
            key_i = lax.broadcasted_iota(jnp.int32, (tk, wide), 0)
            query_i = lax.broadcasted_iota(jnp.int32, (tk, wide), 1) & (tq - 1)
            s = jnp.where(key_i <= query_i, s, -jnp.inf)
        s_ref[...] = s
        return jnp.max(s, axis=0, keepdims=True)

    def accumulate(j, s_ref, ms, maxima):
        m_new = jnp.maximum(ms, maxima)
        alpha = jnp.exp2(ms - m_new)
        p = jnp.exp2(s_ref[...] - m_new).astype(BF16)
        half = wide // 2
        for h in range(2):
            vb = jnp.concatenate([vt_ref[j, h * DIFF_V_DIM:(h + 1) * DIFF_V_DIM, :], ones_rows], axis=0)
            cols = slice(h * half, (h + 1) * half)
            acc_ref[h] = acc_ref[h] * alpha[:, cols] + jnp.dot(vb, p[:, cols], preferred_element_type=F32)
        return m_new

    ms = jnp.full((1, wide), -jnp.inf, F32)
    maxima = scores(qi, s0_ref, True)

    def step(t, prev, s_write, s_read, ms, maxima):
        nxt = scores(t, s_write, False)
        return accumulate(prev, s_read, ms, maxima), nxt

    def pair(u, carry):
        ms, maxima, prev = carry
        ms, maxima = step(2 * u, prev, s1_ref, s0_ref, ms, maxima)
        ms, maxima = step(2 * u + 1, 2 * u, s0_ref, s1_ref, ms, maxima)
        return ms, maxima, 2 * u + 1

    def pairs(n):
        def run(u0, carry):
            for i in range(n):
                carry = pair(u0 + i, carry)
            return carry
        return run

    n_pairs = qi // 2
    group = ATT_UNROLL // 2
    carry = lax.fori_loop(0, n_pairs // group, lambda g, c: pairs(group)(g * group, c), (ms, maxima, qi))
    done = (n_pairs // group) * group
    while group > 1:
        group //= 2
        carry = lax.cond((n_pairs & group) != 0, functools.partial(pairs(group), done), lambda c: c, carry)
        done = done + (n_pairs & group)
    ms, maxima, prev = carry

    def odd_tail():
        ms1, maxima1 = step(qi - 1, prev, s1_ref, s0_ref, ms, maxima)
        accumulate(qi - 1, s1_ref, ms1, maxima1)

    lax.cond(qi % 2 == 1, odd_tail, lambda: (accumulate(prev, s0_ref, ms, maxima), None)[1])

    lamv = lamv_ref[...]
    lam = (jnp.exp(jnp.sum(lamv[0:1] * lamv[1:2], axis=-1, keepdims=True))
           - jnp.exp(jnp.sum(lamv[2:3] * lamv[3:4], axis=-1, keepdims=True)) + lambda_init)
    outs = []
    for h in range(2):
        norm = [acc_ref[h, 0:DIFF_V_DIM, c * tq:(c + 1) * tq]
                * (1.0 / acc_ref[h, DIFF_V_DIM:DIFF_V_DIM + 1, c * tq:(c + 1) * tq]) for c in range(2)]
        o_h = norm[0] - lam * norm[1]
        mean_sq = jnp.mean(o_h * o_h, axis=0, keepdims=True)
        outs.append(o_h * lax.rsqrt(mean_sq + LN_EPS))
    y = jnp.concatenate(outs, axis=0).T * ng_ref[...] * (1.0 - lambda_init)
    o_ref[...] = y.astype(o_ref.dtype)


def _diff_attention(q, k, vt, lamv, ng2, batch, seq, lambda_init):
    t = q.shape[0]
    tq, tk = ATT_TQ, ATT_TK
    nq = seq // tq
    nkb = seq // tk
    n_pairs = DIFF_W // LANES
    n_comp = LANES // DIFF_QK_DIM
    assert tq == tk and tq & (tq - 1) == 0, "square power-of-two blocks"
    return pl.pallas_call(
        functools.partial(_attn_kernel, lambda_init),
        grid=(batch, n_pairs, nq),
        in_specs=[
            pl.BlockSpec((tq, LANES), lambda b, hp, i: (b * nq + i, hp)),
            pl.BlockSpec((seq, LANES), lambda b, hp, i: (b, hp)),
            pl.BlockSpec((nkb, LANES, tk), lambda b, hp, i: (b, hp, 0)),
            _const_spec(lamv.shape),
            _const_spec(ng2.shape),
        ],
        out_specs=pl.BlockSpec((tq, LANES), lambda b, hp, i: (b * nq + i, hp)),
        out_shape=jax.ShapeDtypeStruct((t, DIFF_W), BF16),
        scratch_shapes=[pltpu.VMEM((2, DIFF_V_DIM + ATT_SUM_ROWS, n_comp * tq // 2), F32),
                        pltpu.VMEM((tk, n_comp * tq), F32), pltpu.VMEM((tk, n_comp * tq), F32)],
        compiler_params=_params(3),
        name="diff_attention",
    )(q, k, vt, lamv, ng2)


def _split3(x):
    hi = x.astype(BF16)
    r = x - hi.astype(F32)
    mid = r.astype(BF16)
    lo = (r - mid.astype(F32)).astype(BF16)
    return hi, mid, lo


def _ssd_kernel(xbc_ref, z_ref, dt_ref, cw_ref, cb_ref, dtb_ref, alog_ref, dsk_ref, ng_ref, o_ref,
                ext_ref, state_ref):
    c = pl.program_id(1)
    lc = xbc_ref.shape[0]
    g_w = SSD_HEADS_PER_GROUP * SSD_HEAD_DIM
    n = SSD_STATE

    @pl.when(c == 0)
    def _():
        state_ref[...] = jnp.zeros_like(state_ref)
        ext_ref[0:SUBLANES, :] = jnp.zeros((SUBLANES, CONV_CH), F32)

    xin = xbc_ref[...]
    ext_ref[SUBLANES:, :] = xin
    cw = cw_ref[...]
    conv = cb_ref[...] + cw[CONV_K - 1:CONV_K] * xin
    for j in range(CONV_K - 1):
        off = SUBLANES - (CONV_K - 1) + j
        conv = conv + cw[j:j + 1] * ext_ref[off:off + lc, :]
    ext_ref[0:SUBLANES, :] = xin[lc - SUBLANES:, :]
    act = conv * jax.nn.sigmoid(conv)
    xs = act[:, :SSD_INNER]
    bm = act[:, SSD_INNER:SSD_INNER + SSD_GROUPS * n]
    cm = act[:, SSD_INNER + SSD_GROUPS * n:]

    dt_in = dt_ref[...] + dtb_ref[...]
    dt = jnp.maximum(dt_in, 0.0) + jnp.log(1.0 + jnp.exp(-jnp.abs(dt_in)))
    a_neg = -jnp.exp(alog_ref[...])
    dta = dt * a_neg
    ri = lax.broadcasted_iota(jnp.int32, (lc, lc), 0)
    ci = lax.broadcasted_iota(jnp.int32, (lc, lc), 1)
    causal = ri >= ci
    tri = jnp.where(causal, 1.0, 0.0).astype(BF16)
    acs = None
    for part in _split3(dta):
        term = jnp.dot(tri, part, preferred_element_type=F32)
        acs = term if acs is None else acs + term
    acs_last = acs[lc - 1:lc, :]
    xdt = xs * dt
    decayed = (xdt * jnp.exp(acs_last - acs)).astype(BF16)
    xdt_b = xdt.astype(BF16)
    exp_acs = jnp.exp(acs)
    chunk_decay = jnp.exp(acs_last)

    lane_g = lax.broadcasted_iota(jnp.int32, (1, g_w), 1)
    lane128 = lax.broadcasted_iota(jnp.int32, (1, LANES), 1)
    y_parts = []
    for g in range(SSD_GROUPS):
        bm_g = bm[:, g * n:(g + 1) * n]
        cm_g = cm[:, g * n:(g + 1) * n].astype(BF16)
        cb = lax.dot_general(cm_g, bm_g.astype(BF16), (((1,), (1,)), ((), ())), preferred_element_type=F32)
        xdt_g = xdt_b[:, g * g_w:(g + 1) * g_w]
        y_g = None
        for pair in range(SSD_HEADS_PER_GROUP // 2):
            lo = g * g_w + pair * LANES
            a_pair = acs[:, lo:lo + LANES]
            a_swap = pltpu.roll(a_pair, DIFF_V_DIM, 1)
            a_t = a_pair.T
            for sub in range(2):
                r = 2 * pair + sub
                own = (lane128 < SSD_HEAD_DIM) if sub == 0 else (lane128 >= SSD_HEAD_DIM)
                col = jnp.where(own, a_pair, a_swap)
                rowv = a_t[sub * SSD_HEAD_DIM:sub * SSD_HEAD_DIM + 1, :]
                lmat = jnp.where(causal, jnp.exp(col - rowv), 0.0)
                w = (cb * lmat).astype(BF16)
                sel = (lane_g >= r * SSD_HEAD_DIM) & (lane_g < (r + 1) * SSD_HEAD_DIM)
                contrib = jnp.dot(w, jnp.where(sel, xdt_g, jnp.zeros_like(xdt_g)), preferred_element_type=F32)
                y_g = contrib if y_g is None else y_g + contrib
        prev = state_ref[g]
        y_off = jnp.dot(cm_g, prev.astype(BF16), preferred_element_type=F32) * exp_acs[:, g * g_w:(g + 1) * g_w]
        st = jnp.dot(bm_g.T.astype(BF16), decayed[:, g * g_w:(g + 1) * g_w], preferred_element_type=F32)
        state_ref[g] = prev * chunk_decay[:, g * g_w:(g + 1) * g_w] + st
        y_parts.append(y_g + y_off)
    y = jnp.concatenate(y_parts, axis=1) + xs * dsk_ref[...]
    zz = z_ref[...]
    y = y * (zz * jax.nn.sigmoid(zz))
    y = y * lax.rsqrt(jnp.mean(y * y, axis=-1, keepdims=True) + LN_EPS)
    o_ref[...] = (y * ng_ref[...]).astype(o_ref.dtype)


def _ssd_mixer(xbc, z, dtf, conv_w, conv_b, dtb_f, alog_f, dsk_f, norm_g, batch, seq):
    t = xbc.shape[0]
    nc = seq // CHUNK
    row = lambda w: pl.BlockSpec((CHUNK, w), lambda b, c: (b * nc + c, 0))
    return pl.pallas_call(
        _ssd_kernel,
        grid=(batch, nc),
        in_specs=[row(CONV_CH), row(SSD_INNER), row(SSD_INNER),
                  _const_spec(conv_w.shape), _const_spec(conv_b.shape), _const_spec(dtb_f.shape),
                  _const_spec(alog_f.shape), _const_spec(dsk_f.shape), _const_spec(norm_g.shape)],
        out_specs=row(SSD_INNER),
        out_shape=jax.ShapeDtypeStruct((t, SSD_INNER), BF16),
        scratch_shapes=[pltpu.VMEM((SUBLANES + CHUNK, CONV_CH), F32),
                        pltpu.VMEM((SSD_GROUPS, SSD_STATE, SSD_HEADS_PER_GROUP * SSD_HEAD_DIM), F32)],
        compiler_params=_params(2),
        name="ssd_mixer",
    )(xbc, z, dtf, conv_w, conv_b, dtb_f, alog_f, dsk_f, norm_g)


def _pooled(seq_blk, u_ref, halo_ref, lvl_ref):
    tm = u_ref.shape[0]
    base = 2 * POOL_HALO
    u = u_ref[...]
    halo = halo_ref[...]
    lvl_ref[:, 0:POOL_HALO, :] = jnp.zeros((len(POOL_WINDOWS), POOL_HALO, POOL_W), F32)
    lvl_ref[0, POOL_HALO:base, :] = jnp.where(seq_blk == 0, jnp.zeros_like(halo), halo)
    lvl_ref[0, base:, :] = u
    n_rows = tm + POOL_HALO
    sums = []
    for lv, shift in enumerate((1, 2, 4, 8)):
        cur = lvl_ref[lv, POOL_HALO:, :] + lvl_ref[lv, POOL_HALO - shift:POOL_HALO - shift + n_rows, :]
        if lv + 1 < len(POOL_WINDOWS):
            lvl_ref[lv + 1, POOL_HALO:, :] = cur
        sums.append(cur[POOL_HALO:, :])
    pos = (seq_blk * tm + lax.broadcasted_iota(jnp.int32, (tm, 1), 0) + 1).astype(F32)
    lane = lax.broadcasted_iota(jnp.int32, (1, POOL_W), 1)
    pooled = jnp.zeros((tm, POOL_W), F32)
    for gi, w in enumerate(POOL_WINDOWS):
        mean = sums[gi] / jnp.minimum(pos, float(w))
        pooled = jnp.where((lane >= gi * POOL_GROUP) & (lane < (gi + 1) * POOL_GROUP), mean, pooled)
    return pooled - u


def _mix_ffn_kernel(alpha, blocks_per_seq, u_ref, halo_ref, od_ref, os_ref, h_ref, p_ref, pw_ref, ps_ref, wo_ref,
                    g1_ref, b1_ref, w1_ref, w2_ref, wp_ref, wg_ref, g2_ref, b2_ref, o_ref, lvl_ref):
    seq_blk = pl.program_id(0) % blocks_per_seq
    pooled = _pooled(seq_blk, u_ref, halo_ref, lvl_ref)
    o_pool = jnp.dot(pooled.astype(BF16), pw_ref[...], preferred_element_type=F32) * ps_ref[...]
    cat = jnp.concatenate([o_pool.astype(BF16), od_ref[...], os_ref[...]], axis=1)
    mix = jnp.dot(cat, wo_ref[...], preferred_element_type=F32)
    h = _layer_norm(alpha * h_ref[...] + mix, g1_ref[...], b1_ref[...])

    hb = h.astype(BF16)
    gate = jax.nn.sigmoid(jnp.dot(hb, wg_ref[...], preferred_element_type=F32))
    res = alpha * h + jnp.dot(p_ref[...].astype(BF16), wp_ref[...], preferred_element_type=F32) * gate
    hid = jnp.dot(hb, w1_ref[...], preferred_element_type=F32)
    hid = jnp.square(jnp.maximum(hid, 0.0)).astype(BF16)
    ff = jnp.dot(hid, w2_ref[...], preferred_element_type=F32)
    o_ref[...] = _layer_norm(res + ff, g2_ref[...], b2_ref[...])


def _mix_ffn(u_pool, o_diff, o_ssd, h, p, layer, consts, seq, alpha):
    t, d = h.shape
    tm = ROW_TILE
    blocks_per_seq = seq // tm
    halo_per_tile = tm // POOL_HALO
    row = lambda w: pl.BlockSpec((tm, w), lambda i: (i, 0))
    return pl.pallas_call(
        functools.partial(_mix_ffn_kernel, alpha, blocks_per_seq),
        grid=(t // tm,),
        in_specs=[row(POOL_W),
                  pl.BlockSpec((POOL_HALO, POOL_W), lambda i: (jnp.maximum(i * halo_per_tile - 1, 0), 0)),
                  row(DIFF_W), row(SSD_INNER), row(d),
                  pl.BlockSpec((None, tm, p.shape[-1]), lambda i: (layer, i, 0))]
                 + [_const_spec(c.shape) for c in consts],
        out_specs=row(d),
        out_shape=jax.ShapeDtypeStruct((t, d), F32),
        scratch_shapes=[pltpu.VMEM((len(POOL_WINDOWS), tm + 2 * POOL_HALO, POOL_W), F32)],
        compiler_params=_params(1),
        name="mix_ffn",
    )(u_pool, u_pool, o_diff, o_ssd, h, p, *consts)


def _row(v):
    return v.reshape(1, -1).astype(F32)


def _per_head_lanes(v):
    return jnp.repeat(v.astype(F32), SSD_HEAD_DIM).reshape(1, SSD_INNER)


def kernel(x, p, ln_in_g, ln_in_b, w_in, pool_w, pool_scale, lam_q1, lam_k1, lam_q2, lam_k2, diff_norm_g, conv_w, conv_b, dt_bias, a_log, d_skip, ssd_norm_g, w_out, ln1_g, ln1_b, w_ff1, w_ff2, w_ple, w_ple_gate, ln2_g, ln2_b):
    batch, seq, d_model = x.shape
    depth = w_in.shape[0]
    t = batch * seq
    alpha = (2 * depth) ** 0.25
    h = x.reshape(t, d_model)
    p2 = p.reshape(depth, t, p.shape[-1])
    for i in range(depth):
        lambda_init = 0.8 - 0.6 * math.exp(-0.3 * i)
        w = w_in[i]
        w_aug = jnp.concatenate([w[:, :_OFF_DT], jnp.repeat(w[:, _OFF_DT:], SSD_HEAD_DIM, axis=1)], axis=1).astype(BF16)
        outs = _in_projection(h, _row(ln_in_g), _row(ln_in_b), w_aug, apply_ln=(i == 0))
        if i == 0:
            h, outs = outs[0], outs[1:]
        u_pool, q, k, vt, z, xbc, dtf = outs

        lamv = jnp.zeros((4, LANES), F32).at[:, :DIFF_QK_DIM].set(
            jnp.stack([lam_q1[i], lam_k1[i], lam_q2[i], lam_k2[i]]).astype(F32))
        ng2 = jnp.tile(diff_norm_g[i].astype(F32), LANES // DIFF_V_DIM).reshape(1, LANES)
        o_diff = _diff_attention(q, k, vt, lamv, ng2, batch, seq, lambda_init)

        o_ssd = _ssd_mixer(xbc, z, dtf, conv_w[i].astype(F32), _row(conv_b[i]), _per_head_lanes(dt_bias[i]),
                           _per_head_lanes(a_log[i]), _per_head_lanes(d_skip[i]), _row(ssd_norm_g[i]), batch, seq)

        pool_bd = jax.scipy.linalg.block_diag(*[pool_w[i, g] for g in range(len(POOL_WINDOWS))]).astype(BF16)
        consts = (pool_bd, _row(pool_scale[i]), w_out[i].astype(BF16), _row(ln1_g[i]), _row(ln1_b[i]),
                  w_ff1[i].astype(BF16), w_ff2[i].astype(BF16), w_ple[i].astype(BF16), w_ple_gate[i].astype(BF16),
                  _row(ln2_g[i]), _row(ln2_b[i]))
        h = _mix_ffn(u_pool, o_diff, o_ssd, h, p2, i, consts, seq, alpha)
    return h.reshape(batch, seq, d_model)
```

```python
import functools
import math

import jax
import jax.numpy as jnp
from jax import lax
from jax.experimental import pallas as pl
from jax.experimental.pallas import tpu as pltpu

F32 = jnp.float32
BF16 = jnp.bfloat16

POOL_WINDOWS = (2, 4, 8, 16)
POOL_GROUP = 64
POOL_W = POOL_GROUP * len(POOL_WINDOWS)
DIFF_HEADS = 4
DIFF_QK_DIM = 32
DIFF_V_DIM = 2 * DIFF_QK_DIM
DIFF_W = DIFF_HEADS * DIFF_V_DIM
SSD_INNER = 512
SSD_HEAD_DIM = 64
SSD_HEADS = SSD_INNER // SSD_HEAD_DIM
SSD_GROUPS = 2
SSD_HEADS_PER_GROUP = SSD_HEADS // SSD_GROUPS
SSD_STATE = 128
CONV_K = 4
CONV_CH = SSD_INNER + 2 * SSD_GROUPS * SSD_STATE
CHUNK = 128
LN_EPS = 1e-5
LOG2E = 1.4426950408889634

LANES = 128
SUBLANES = 8
VMEM_LIMIT = 56 * 1024 * 1024

ROW_TILE = 512
ATT_TQ = 256
ATT_TK = 256
ATT_UNROLL = 8
ATT_SUM_ROWS = 16
POOL_HALO = 16

_OFF_POOL = 0
_OFF_Q = _OFF_POOL + POOL_W
_OFF_K = _OFF_Q + DIFF_W
_OFF_V = _OFF_K + DIFF_W
_OFF_Z = _OFF_V + DIFF_W
_OFF_XBC = _OFF_Z + SSD_INNER
_OFF_DT = _OFF_XBC + CONV_CH
_AUG_W = _OFF_DT + SSD_INNER


def _layer_norm(x, g, b):
    mu = jnp.mean(x, axis=-1, keepdims=True)
    xc = x - mu
    var = jnp.mean(xc * xc, axis=-1, keepdims=True)
    return xc * lax.rsqrt(var + LN_EPS) * g + b


def _const_spec(shape):
    nd = len(shape)
    return pl.BlockSpec(shape, lambda *_: (0,) * nd, pipeline_mode=pl.Buffered(1))


def _params(n_axes):
    return pltpu.CompilerParams(dimension_semantics=("arbitrary",) * n_axes, vmem_limit_bytes=VMEM_LIMIT)


def _inproj_kernel(apply_ln, qscale, x_ref, g_ref, b_ref, w_ref, *out_refs):
    if apply_ln:
        h_ref, up_ref, q_ref, k_ref, vt_ref, z_ref, xbc_ref, dt_ref = out_refs
        x = _layer_norm(x_ref[...], g_ref[...], b_ref[...])
        h_ref[...] = x
    else:
        up_ref, q_ref, k_ref, vt_ref, z_ref, xbc_ref, dt_ref = out_refs
        x = x_ref[...]
    xb = x.astype(BF16)

    def proj(lo, width):
        return jnp.dot(xb, w_ref[:, lo:lo + width], preferred_element_type=F32)

    up_ref[...] = proj(_OFF_POOL, POOL_W)
    q_ref[...] = (proj(_OFF_Q, DIFF_W) * qscale).astype(BF16)
    k_ref[...] = proj(_OFF_K, DIFF_W).astype(BF16)
    v = proj(_OFF_V, DIFF_W)
    tk = vt_ref.shape[-1]
    for s in range(vt_ref.shape[0]):
        vt_ref[s] = v[s * tk:(s + 1) * tk, :].T.astype(BF16)
    z_ref[...] = proj(_OFF_Z, SSD_INNER)
    xbc_ref[...] = proj(_OFF_XBC, CONV_CH)
    dt_ref[...] = proj(_OFF_DT, SSD_INNER)


def _in_projection(x, ln_g, ln_b, w_aug, apply_ln):
    t, d = x.shape
    tm = ROW_TILE
    nsub = tm // ATT_TK
    qscale = DIFF_QK_DIM ** -0.5 * LOG2E
    row = lambda w: pl.BlockSpec((tm, w), lambda i: (i, 0))
    out_shape = [
        jax.ShapeDtypeStruct((t, POOL_W), F32),
        jax.ShapeDtypeStruct((t, DIFF_W), BF16),
        jax.ShapeDtypeStruct((t, DIFF_W), BF16),
        jax.ShapeDtypeStruct((t // ATT_TK, DIFF_W, ATT_TK), BF16),
        jax.ShapeDtypeStruct((t, SSD_INNER), F32),
        jax.ShapeDtypeStruct((t, CONV_CH), F32),
        jax.ShapeDtypeStruct((t, SSD_INNER), F32),
    ]
    out_specs = [row(POOL_W), row(DIFF_W), row(DIFF_W),
                 pl.BlockSpec((nsub, DIFF_W, ATT_TK), lambda i: (i, 0, 0)),
                 row(SSD_INNER), row(CONV_CH), row(SSD_INNER)]
    if apply_ln:
        out_shape = [jax.ShapeDtypeStruct((t, d), F32)] + out_shape
        out_specs = [row(d)] + out_specs
    return pl.pallas_call(
        functools.partial(_inproj_kernel, apply_ln, qscale),
        grid=(t // tm,),
        in_specs=[row(d), _const_spec((1, d)), _const_spec((1, d)), _const_spec(w_aug.shape)],
        out_specs=out_specs,
        out_shape=out_shape,
        compiler_params=_params(1),
        name="in_proj_ln" if apply_ln else "in_proj",
    )(x, ln_g, ln_b, w_aug)


def _attn_kernel(lambda_init, tq, q_ref, k_ref, vt_ref, lamv_ref, ng_ref, o_ref, acc_ref, s0_ref, s1_ref):
    def tile(qi, carry):
        rows = pl.ds(pl.multiple_of(qi * tq, tq), tq)
        _attn_tile(lambda_init, qi, q_ref.at[rows, :], k_ref, vt_ref, lamv_ref, ng_ref, o_ref.at[rows, :],
                   acc_ref, s0_ref, s1_ref)
        return carry

    lax.fori_loop(0, q_ref.shape[0] // tq, tile, 0)


def _attn_tile(lambda_init, qi, q_ref, k_ref, vt_ref, lamv_ref, ng_ref, o_ref, acc_ref, s0_ref, s1_ref):
    tq = q_ref.shape[0]
    tk = vt_ref.shape[-1]
    n_comp = LANES // DIFF_QK_DIM
    qb = q_ref[...]
    lane = lax.broadcasted_iota(jnp.int32, (1, LANES), 1)
    zero = jnp.zeros_like(qb)
    q4 = jnp.concatenate(
        [jnp.where((lane >= DIFF_QK_DIM * c) & (lane < DIFF_QK_DIM * (c + 1)), qb, zero) for c in range(n_comp)],
        axis=0)
    wide = n_comp * tq
    acc_ref[...] = jnp.zeros_like(acc_ref)
    ones_rows = jnp.ones((ATT_SUM_ROWS, tk), BF16)

    def scores(j, s_ref, masked):
        kb = k_ref[pl.ds(pl.multiple_of(j * tk, tk), tk), :]
        s = lax.dot_general(kb, q4, (((1,), (1,)), ((), ())), preferred_element_type=F32)
        if masked:
            key_i = lax.broadcasted_iota(jnp.int32, (tk, wide), 0)
            query_i = lax.broadcasted_iota(jnp.int32, (tk, wide), 1) & (tq - 1)
            s = jnp.where(key_i <= query_i, s, -jnp.inf)
        s_ref[...] = s
        return jnp.max(s, axis=0, keepdims=True)

    def accumulate(j, s_ref, ms, maxima):
        m_new = jnp.maximum(ms, maxima)
        alpha = jnp.exp2(ms - m_new)
        p = jnp.exp2(s_ref[...] - m_new).astype(BF16)
        half = wide // 2
        for h in range(2):
            vb = jnp.concatenate([vt_ref[j, h * DIFF_V_DIM:(h + 1) * DIFF_V_DIM, :], ones_rows], axis=0)
            cols = slice(h * half, (h + 1) * half)
            acc_ref[h] = acc_ref[h] * alpha[:, cols] + jnp.dot(vb, p[:, cols], preferred_element_type=F32)
        return m_new

    ms = jnp.full((1, wide), -jnp.inf, F32)
    maxima = scores(qi, s0_ref, True)

    def step(t, prev, s_write, s_read, ms, maxima):
        nxt = scores(t, s_write, False)
        return accumulate(prev, s_read, ms, maxima), nxt

    def pair(u, carry):
        ms, maxima, prev = carry
        ms, maxima = step(2 * u, prev, s1_ref, s0_ref, ms, maxima)
        ms, maxima = step(2 * u + 1, 2 * u, s0_ref, s1_ref, ms, maxima)
        return ms, maxima, 2 * u + 1

    def pairs(n):
        def run(u0, carry):
            for i in range(n):
                carry = pair(u0 + i, carry)
            return carry
        return run

    n_pairs = qi // 2
    group = ATT_UNROLL // 2
    carry = lax.fori_loop(0, n_pairs // group, lambda g, c: pairs(group)(g * group, c), (ms, maxima, qi))
    done = (n_pairs // group) * group
    while group > 1:
        group //= 2
        carry = lax.cond((n_pairs & group) != 0, functools.partial(pairs(group), done), lambda c: c, carry)
        done = done + (n_pairs & group)
    ms, maxima, prev = carry

    def odd_tail():
        ms1, maxima1 = step(qi - 1, prev, s1_ref, s0_ref, ms, maxima)
        accumulate(qi - 1, s1_ref, ms1, maxima1)

    lax.cond(qi % 2 == 1, odd_tail, lambda: (accumulate(prev, s0_ref, ms, maxima), None)[1])

    lamv = lamv_ref[...]
    lam = (jnp.exp(jnp.sum(lamv[0:1] * lamv[1:2], axis=-1, keepdims=True))
           - jnp.exp(jnp.sum(lamv[2:3] * lamv[3:4], axis=-1, keepdims=True)) + lambda_init)
    outs = []
    for h in range(2):
        norm = [acc_ref[h, 0:DIFF_V_DIM, c * tq:(c + 1) * tq]
                * (1.0 / acc_ref[h, DIFF_V_DIM:DIFF_V_DIM + 1, c * tq:(c + 1) * tq]) for c in range(2)]
        o_h = norm[0] - lam * norm[1]
        mean_sq = jnp.mean(o_h * o_h, axis=0, keepdims=True)
        outs.append(o_h * lax.rsqrt(mean_sq + LN_EPS))
    y = jnp.concatenate(outs, axis=0).T * ng_ref[...] * (1.0 - lambda_init)
    o_ref[...] = y.astype(o_ref.dtype)


def _diff_attention(q, k, vt, lamv, ng2, batch, seq, lambda_init):
    t = q.shape[0]
    tq, tk = ATT_TQ, ATT_TK
    nkb = seq // tk
    n_pairs = DIFF_W // LANES
    n_comp = LANES // DIFF_QK_DIM
    assert tq == tk and tq & (tq - 1) == 0, "square power-of-two blocks"
    return pl.pallas_call(
        functools.partial(_attn_kernel, lambda_init, tq),
        grid=(batch, n_pairs),
        in_specs=[
            pl.BlockSpec((seq, LANES), lambda b, hp: (b, hp)),
            pl.BlockSpec((seq, LANES), lambda b, hp: (b, hp)),
            pl.BlockSpec((nkb, LANES, tk), lambda b, hp: (b, hp, 0)),
            _const_spec(lamv.shape),
            _const_spec(ng2.shape),
        ],
        out_specs=pl.BlockSpec((seq, LANES), lambda b, hp: (b, hp)),
        out_shape=jax.ShapeDtypeStruct((t, DIFF_W), BF16),
        scratch_shapes=[pltpu.VMEM((2, DIFF_V_DIM + ATT_SUM_ROWS, n_comp * tq // 2), F32),
                        pltpu.VMEM((tk, n_comp * tq), F32), pltpu.VMEM((tk, n_comp * tq), F32)],
        compiler_params=_params(2),
        name="diff_attention",
    )(q, k, vt, lamv, ng2)


def _split3(x):
    hi = x.astype(BF16)
    r = x - hi.astype(F32)
    mid = r.astype(BF16)
    lo = (r - mid.astype(F32)).astype(BF16)
    return hi, mid, lo


def _ssd_kernel(xbc_ref, z_ref, dt_ref, cw_ref, cb_ref, dtb_ref, alog_ref, dsk_ref, ng_ref, o_ref,
                ext_ref, state_ref):
    c = pl.program_id(1)
    lc = xbc_ref.shape[0]
    g_w = SSD_HEADS_PER_GROUP * SSD_HEAD_DIM
    n = SSD_STATE

    @pl.when(c == 0)
    def _():
        state_ref[...] = jnp.zeros_like(state_ref)
        ext_ref[0:SUBLANES, :] = jnp.zeros((SUBLANES, CONV_CH), F32)

    xin = xbc_ref[...]
    ext_ref[SUBLANES:, :] = xin
    cw = cw_ref[...]
    conv = cb_ref[...] + cw[CONV_K - 1:CONV_K] * xin
    for j in range(CONV_K - 1):
        off = SUBLANES - (CONV_K - 1) + j
        conv = conv + cw[j:j + 1] * ext_ref[off:off + lc, :]
    ext_ref[0:SUBLANES, :] = xin[lc - SUBLANES:, :]
    act = conv * jax.nn.sigmoid(conv)
    xs = act[:, :SSD_INNER]
    bm = act[:, SSD_INNER:SSD_INNER + SSD_GROUPS * n]
    cm = act[:, SSD_INNER + SSD_GROUPS * n:]

    dt_in = dt_ref[...] + dtb_ref[...]
    dt = jnp.maximum(dt_in, 0.0) + jnp.log(1.0 + jnp.exp(-jnp.abs(dt_in)))
    a_neg = -jnp.exp(alog_ref[...])
    dta = dt * a_neg
    ri = lax.broadcasted_iota(jnp.int32, (lc, lc), 0)
    ci = lax.broadcasted_iota(jnp.int32, (lc, lc), 1)
    causal = ri >= ci
    tri = jnp.where(causal, 1.0, 0.0).astype(BF16)
    acs = None
    for part in _split3(dta):
        term = jnp.dot(tri, part, preferred_element_type=F32)
        acs = term if acs is None else acs + term
    acs_last = acs[lc - 1:lc, :]
    xdt = xs * dt
    decayed = (xdt * jnp.exp(acs_last - acs)).astype(BF16)
    xdt_b = xdt.astype(BF16)
    exp_acs = jnp.exp(acs)
    chunk_decay = jnp.exp(acs_last)

    lane_g = lax.broadcasted_iota(jnp.int32, (1, g_w), 1)
    lane128 = lax.broadcasted_iota(jnp.int32, (1, LANES), 1)
    y_parts = []
    for g in range(SSD_GROUPS):
        bm_g = bm[:, g * n:(g + 1) * n]
        cm_g = cm[:, g * n:(g + 1) * n].astype(BF16)
        cb = lax.dot_general(cm_g, bm_g.astype(BF16), (((1,), (1,)), ((), ())), preferred_element_type=F32)
        xdt_g = xdt_b[:, g * g_w:(g + 1) * g_w]
        y_g = None
        for pair in range(SSD_HEADS_PER_GROUP // 2):
            lo = g * g_w + pair * LANES
            a_pair = acs[:, lo:lo + LANES]
            a_swap = pltpu.roll(a_pair, DIFF_V_DIM, 1)
            a_t = a_pair.T
            for sub in range(2):
                r = 2 * pair + sub
                own = (lane128 < SSD_HEAD_DIM) if sub == 0 else (lane128 >= SSD_HEAD_DIM)
                col = jnp.where(own, a_pair, a_swap)
                rowv = a_t[sub * SSD_HEAD_DIM:sub * SSD_HEAD_DIM + 1, :]
                lmat = jnp.where(causal, jnp.exp(col - rowv), 0.0)
                w = (cb * lmat).astype(BF16)
                sel = (lane_g >= r * SSD_HEAD_DIM) & (lane_g < (r + 1) * SSD_HEAD_DIM)
                contrib = jnp.dot(w, jnp.where(sel, xdt_g, jnp.zeros_like(xdt_g)), preferred_element_type=F32)
                y_g = contrib if y_g is None else y_g + contrib
        prev = state_ref[g]
        y_off = jnp.dot(cm_g, prev.astype(BF16), preferred_element_type=F32) * exp_acs[:, g * g_w:(g + 1) * g_w]
        st = jnp.dot(bm_g.T.astype(BF16), decayed[:, g * g_w:(g + 1) * g_w], preferred_element_type=F32)
        state_ref[g] = prev * chunk_decay[:, g * g_w:(g + 1) * g_w] + st
        y_parts.append(y_g + y_off)
    y = jnp.concatenate(y_parts, axis=1) + xs * dsk_ref[...]
    zz = z_ref[...]
    y = y * (zz * jax.nn.sigmoid(zz))
    y = y * lax.rsqrt(jnp.mean(y * y, axis=-1, keepdims=True) + LN_EPS)
    o_ref[...] = (y * ng_ref[...]).astype(o_ref.dtype)


def _ssd_mixer(xbc, z, dtf, conv_w, conv_b, dtb_f, alog_f, dsk_f, norm_g, batch, seq):
    t = xbc.shape[0]
    nc = seq // CHUNK
    row = lambda w: pl.BlockSpec((CHUNK, w), lambda b, c: (b * nc + c, 0))
    return pl.pallas_call(
        _ssd_kernel,
        grid=(batch, nc),
        in_specs=[row(CONV_CH), row(SSD_INNER), row(SSD_INNER),
                  _const_spec(conv_w.shape), _const_spec(conv_b.shape), _const_spec(dtb_f.shape),
                  _const_spec(alog_f.shape), _const_spec(dsk_f.shape), _const_spec(norm_g.shape)],
        out_specs=row(SSD_INNER),
        out_shape=jax.ShapeDtypeStruct((t, SSD_INNER), BF16),
        scratch_shapes=[pltpu.VMEM((SUBLANES + CHUNK, CONV_CH), F32),
                        pltpu.VMEM((SSD_GROUPS, SSD_STATE, SSD_HEADS_PER_GROUP * SSD_HEAD_DIM), F32)],
        compiler_params=_params(2),
        name="ssd_mixer",
    )(xbc, z, dtf, conv_w, conv_b, dtb_f, alog_f, dsk_f, norm_g)


def _pooled(seq_blk, u_ref, halo_ref, lvl_ref):
    tm = u_ref.shape[0]
    base = 2 * POOL_HALO
    u = u_ref[...]
    halo = halo_ref[...]
    lvl_ref[:, 0:POOL_HALO, :] = jnp.zeros((len(POOL_WINDOWS), POOL_HALO, POOL_W), F32)
    lvl_ref[0, POOL_HALO:base, :] = jnp.where(seq_blk == 0, jnp.zeros_like(halo), halo)
    lvl_ref[0, base:, :] = u
    n_rows = tm + POOL_HALO
    sums = []
    for lv, shift in enumerate((1, 2, 4, 8)):
        cur = lvl_ref[lv, POOL_HALO:, :] + lvl_ref[lv, POOL_HALO - shift:POOL_HALO - shift + n_rows, :]
        if lv + 1 < len(POOL_WINDOWS):
            lvl_ref[lv + 1, POOL_HALO:, :] = cur
        sums.append(cur[POOL_HALO:, :])
    pos = (seq_blk * tm + lax.broadcasted_iota(jnp.int32, (tm, 1), 0) + 1).astype(F32)
    lane = lax.broadcasted_iota(jnp.int32, (1, POOL_W), 1)
    pooled = jnp.zeros((tm, POOL_W), F32)
    for gi, w in enumerate(POOL_WINDOWS):
        mean = sums[gi] / jnp.minimum(pos, float(w))
        pooled = jnp.where((lane >= gi * POOL_GROUP) & (lane < (gi + 1) * POOL_GROUP), mean, pooled)
    return pooled - u


def _mix_ffn_kernel(alpha, blocks_per_seq, u_ref, halo_ref, od_ref, os_ref, h_ref, p_ref, pw_ref, ps_ref, wo_ref,
                    g1_ref, b1_ref, w1_ref, w2_ref, wp_ref, wg_ref, g2_ref, b2_ref, o_ref, lvl_ref):
    seq_blk = pl.program_id(0) % blocks_per_seq
    pooled = _pooled(seq_blk, u_ref, halo_ref, lvl_ref)
    o_pool = jnp.dot(pooled.astype(BF16), pw_ref[...], preferred_element_type=F32) * ps_ref[...]
    cat = jnp.concatenate([o_pool.astype(BF16), od_ref[...], os_ref[...]], axis=1)
    mix = jnp.dot(cat, wo_ref[...], preferred_element_type=F32)
    h = _layer_norm(alpha * h_ref[...] + mix, g1_ref[...], b1_ref[...])

    hb = h.astype(BF16)
    gate = jax.nn.sigmoid(jnp.dot(hb, wg_ref[...], preferred_element_type=F32))
    res = alpha * h + jnp.dot(p_ref[...].astype(BF16), wp_ref[...], preferred_element_type=F32) * gate
    hid = jnp.dot(hb, w1_ref[...], preferred_element_type=F32)
    hid = jnp.square(jnp.maximum(hid, 0.0)).astype(BF16)
    ff = jnp.dot(hid, w2_ref[...], preferred_element_type=F32)
    o_ref[...] = _layer_norm(res + ff, g2_ref[...], b2_ref[...])


def _mix_ffn(u_pool, o_diff, o_ssd, h, p, layer, consts, seq, alpha):
    t, d = h.shape
    tm = ROW_TILE
    blocks_per_seq = seq // tm
    halo_per_tile = tm // POOL_HALO
    row = lambda w: pl.BlockSpec((tm, w), lambda i: (i, 0))
    return pl.pallas_call(
        functools.partial(_mix_ffn_kernel, alpha, blocks_per_seq),
        grid=(t // tm,),
        in_specs=[row(POOL_W),
                  pl.BlockSpec((POOL_HALO, POOL_W), lambda i: (jnp.maximum(i * halo_per_tile - 1, 0), 0)),
                  row(DIFF_W), row(SSD_INNER), row(d),
                  pl.BlockSpec((None, tm, p.shape[-1]), lambda i: (layer, i, 0))]
                 + [_const_spec(c.shape) for c in consts],
        out_specs=row(d),
        out_shape=jax.ShapeDtypeStruct((t, d), F32),
        scratch_shapes=[pltpu.VMEM((len(POOL_WINDOWS), tm + 2 * POOL_HALO, POOL_W), F32)],
        compiler_params=_params(1),
        name="mix_ffn",
    )(u_pool, u_pool, o_diff, o_ssd, h, p, *consts)


def _row(v):
    return v.reshape(1, -1).astype(F32)


def _per_head_lanes(v):
    return jnp.repeat(v.astype(F32), SSD_HEAD_DIM).reshape(1, SSD_INNER)


def kernel(x, p, ln_in_g, ln_in_b, w_in, pool_w, pool_scale, lam_q1, lam_k1, lam_q2, lam_k2, diff_norm_g, conv_w, conv_b, dt_bias, a_log, d_skip, ssd_norm_g, w_out, ln1_g, ln1_b, w_ff1, w_ff2, w_ple, w_ple_gate, ln2_g, ln2_b):
    batch, seq, d_model = x.shape
    depth = w_in.shape[0]
    t = batch * seq
    alpha = (2 * depth) ** 0.25
    h = x.reshape(t, d_model)
    p2 = p.reshape(depth, t, p.shape[-1])
    for i in range(depth):
        lambda_init = 0.8 - 0.6 * math.exp(-0.3 * i)
        w = w_in[i]
        w_aug = jnp.concatenate([w[:, :_OFF_DT], jnp.repeat(w[:, _OFF_DT:], SSD_HEAD_DIM, axis=1)], axis=1).astype(BF16)
        outs = _in_projection(h, _row(ln_in_g), _row(ln_in_b), w_aug, apply_ln=(i == 0))
        if i == 0:
            h, outs = outs[0], outs[1:]
        u_pool, q, k, vt, z, xbc, dtf = outs

        lamv = jnp.zeros((4, LANES), F32).at[:, :DIFF_QK_DIM].set(
            jnp.stack([lam_q1[i], lam_k1[i], lam_q2[i], lam_k2[i]]).astype(F32))
        ng2 = jnp.tile(diff_norm_g[i].astype(F32), LANES // DIFF_V_DIM).reshape(1, LANES)
        o_diff = _diff_attention(q, k, vt, lamv, ng2, batch, seq, lambda_init)

        o_ssd = _ssd_mixer(xbc, z, dtf, conv_w[i].astype(F32), _row(conv_b[i]), _per_head_lanes(dt_bias[i]),
                           _per_head_lanes(a_log[i]), _per_head_lanes(d_skip[i]), _row(ssd_norm_g[i]), batch, seq)

        pool_bd = jax.scipy.linalg.block_diag(*[pool_w[i, g] for g in range(len(POOL_WINDOWS))]).astype(BF16)
        consts = (pool_bd, _row(pool_scale[i]), w_out[i].astype(BF16), _row(ln1_g[i]), _row(ln1_b[i]),
                  w_ff1[i].astype(BF16), w_ff2[i].astype(BF16), w_ple[i].astype(BF16), w_ple_gate[i].astype(BF16),
                  _row(ln2_g[i]), _row(ln2_b[i]))
        h = _mix_ffn(u_pool, o_diff, o_ssd, h, p2, i, consts, seq, alpha)
    return h.reshape(batch, seq, d_model)
```
